```python
import math
import jax, jax.numpy as jnp
from jax import lax
import numpy as np

D_MODEL = 2048
BATCH = 4
SEQ = 4096
DEPTH = 2

HEAD_DIM = 128
DIL_PATTERNS = ((128, 1), (512, 4), (2048, 16))
N_DIL_GROUPS = 3
HEADS_PER_DIL_GROUP = 4
N_HEADS_A = N_DIL_GROUPS * HEADS_PER_DIL_GROUP
WIDTH_A = N_HEADS_A * HEAD_DIM
WIDTH_A_OUT = HEADS_PER_DIL_GROUP * HEAD_DIM
BAND_BLOCK = 128
N_HEADS_B = 8
WIDTH_B = N_HEADS_B * HEAD_DIM
N_IDX_HEADS = 16
IDX_DIM = 64
TOPK_MAX = 256
QUERY_BLOCK = 128
N_HEADS_TOTAL = N_HEADS_A + N_HEADS_B
REL_BUCKETS = 32
REL_MAX_DIST = 2048
D_FF = -(-8 * D_MODEL // (3 * 256)) * 256
RMS_EPS = 1e-6
PROJ_WIDTHS = (WIDTH_A, WIDTH_A, WIDTH_A, WIDTH_B, WIDTH_B, WIDTH_B,
               N_IDX_HEADS * IDX_DIM, IDX_DIM, N_IDX_HEADS, D_MODEL, D_MODEL)
D_IN = 3 * WIDTH_A + 3 * WIDTH_B + N_IDX_HEADS * IDX_DIM + IDX_DIM + N_IDX_HEADS + 2 * D_MODEL

kernel_name = "gated_dilated_dsa_hybrid"


def rms_norm(x, g):
    xf = x.astype(jnp.float32)
    y = xf * lax.rsqrt(jnp.mean(xf * xf, axis=-1, keepdims=True) + RMS_EPS)
    return (y * g.astype(jnp.float32)).astype(x.dtype)


def t5_bucket(dist):
    n = jnp.maximum(dist, 0)
    max_exact = REL_BUCKETS // 2
    nf = jnp.maximum(n, 1).astype(jnp.float32)
    large = max_exact + (jnp.log(nf / max_exact) / math.log(REL_MAX_DIST / max_exact)
                         * (REL_BUCKETS - max_exact)).astype(jnp.int32)
    large = jnp.minimum(large, REL_BUCKETS - 1)
    return jnp.where(n < max_exact, n, large)


def proj_split_points():
    pts, acc = [], 0
    for w in PROJ_WIDTHS[:-1]:
        acc += w
        pts.append(acc)
    return pts


def dilated_window_attention(q, k, v, dilation, steps, bias_table):
    B, S, H, Dh = q.shape
    span = dilation * BAND_BLOCK
    n_blk = -(-S // span)
    s_pad = n_blk * span

    def to_strided(a):
        a = jnp.pad(a, ((0, 0), (0, s_pad - S), (0, 0), (0, 0)))
        a = a.reshape(B, n_blk, BAND_BLOCK, dilation, H, Dh)
        return a.transpose(0, 3, 4, 1, 2, 5)

    def with_prev(a):
        prev = jnp.pad(a, ((0, 0), (0, 0), (0, 0), (1, 0), (0, 0), (0, 0)))[:, :, :, :-1]
        return jnp.concatenate([prev, a], axis=4)

    qs = to_strided(q)
    kb = with_prev(to_strided(k))
    vb = with_prev(to_strided(v))

    qi = jnp.arange(BAND_BLOCK)[:, None]
    kj = jnp.arange(2 * BAND_BLOCK)[None, :]
    delta = qi + BAND_BLOCK - kj
    band = (delta >= 0) & (delta <= steps)
    has_prev = (jnp.arange(n_blk) > 0)[:, None, None]
    mask = band[None] & (has_prev | (kj >= BAND_BLOCK)[None])
    bias = bias_table[t5_bucket(delta * dilation)].astype(jnp.float32).transpose(2, 0, 1)

    s = jnp.einsum('brhnqe,brhnke->brhnqk', qs, kb,
                   preferred_element_type=jnp.float32) * (Dh ** -0.5)
    s = s + bias[None, None, :, None]
    s = jnp.where(mask[None, None, None], s, -jnp.inf)
    m = jnp.max(s, axis=-1, keepdims=True)
    p = jnp.exp(s - m)
    l = jnp.sum(p, axis=-1)
    o = jnp.einsum('brhnqk,brhnke->brhnqe', p.astype(v.dtype), vb,
                   preferred_element_type=jnp.float32) / l[..., None]
    lse = m[..., 0] + jnp.log(l)
    o = o.transpose(0, 3, 4, 1, 2, 5).reshape(B, s_pad, H, Dh)[:, :S]
    lse = lse.transpose(0, 3, 4, 1, 2).reshape(B, s_pad, H)[:, :S]
    return o, lse


def dsa_attention(q, k, v, q_idx, k_idx, w_idx, bias_table):
    B, S, H, Dh = q.shape
    topk = min(TOPK_MAX, S // 4)
    nb = S // QUERY_BLOCK
    key_pos = jnp.arange(S)

    def blocks(a):
        return jnp.moveaxis(a.reshape((B, nb, QUERY_BLOCK) + a.shape[2:]), 1, 0)

    def one_block(args):
        qb, qib, wib, t0 = args
        tq = t0 + jnp.arange(QUERY_BLOCK)
        dots = jnp.einsum('bqhe,bse->bqhs', qib, k_idx,
                          preferred_element_type=jnp.float32) * (IDX_DIM ** -0.5)
        score = jnp.einsum('bqhs,bqh->bqs', jax.nn.relu(dots),
                           wib.astype(jnp.float32) * (N_IDX_HEADS ** -0.5))
        score = jnp.where(key_pos[None, None, :] <= tq[None, :, None], score, -jnp.inf)
        _, sel = lax.top_k(score, topk)
        valid = sel <= tq[None, :, None]
        kg = jax.vmap(lambda a, i: a[i])(k, sel)
        vg = jax.vmap(lambda a, i: a[i])(v, sel)
        bias = bias_table[t5_bucket(tq[None, :, None] - sel)].astype(jnp.float32)
        s = jnp.einsum('bqhe,bqkhe->bhqk', qb, kg,
                       preferred_element_type=jnp.float32) * (Dh ** -0.5)
        s = s + bias.transpose(0, 3, 1, 2)
        s = jnp.where(valid[:, None], s, -jnp.inf)
        p = jax.nn.softmax(s, axis=-1)
        return jnp.einsum('bhqk,bqkhe->bqhe', p.astype(v.dtype), vg,
                          preferred_element_type=jnp.float32)

    t0s = jnp.arange(nb, dtype=jnp.int32) * QUERY_BLOCK
    out = lax.map(one_block, (blocks(q), blocks(q_idx), blocks(w_idx), t0s))
    return jnp.moveaxis(out, 0, 1).reshape(B, S, H * Dh)


def setup_inputs(seed: int = 0) -> dict:
    key = jax.random.key(seed)
    ks = jax.random.split(key, 13)
    f32 = jnp.float32

    def nrm(k, shape, fan_in):
        return jax.random.normal(k, shape, f32) * (fan_in ** -0.5)

    return {
        "x": jax.random.normal(ks[0], (BATCH, SEQ, D_MODEL), f32),
        "w_in": nrm(ks[1], (DEPTH, D_MODEL, D_IN), D_MODEL),
        "w_out_a": nrm(ks[2], (DEPTH, WIDTH_A_OUT, D_MODEL), WIDTH_A_OUT),
        "w_out_b": nrm(ks[3], (DEPTH, WIDTH_B, D_MODEL), WIDTH_B),
        "w_out": nrm(ks[4], (DEPTH, D_MODEL, D_MODEL), D_MODEL),
        "w_ff_gate": nrm(ks[5], (DEPTH, D_MODEL, D_FF), D_MODEL),
        "w_ff_up": nrm(ks[6], (DEPTH, D_MODEL, D_FF), D_MODEL),
        "w_ff_down": nrm(ks[7], (DEPTH, D_FF, D_MODEL), D_FF),
        "g_mix": 1.0 + 0.02 * jax.random.normal(ks[8], (DEPTH, D_MODEL), f32),
        "g_ffn": 1.0 + 0.02 * jax.random.normal(ks[9], (DEPTH, D_MODEL), f32),
        "g_final": 1.0 + 0.02 * jax.random.normal(ks[10], (D_MODEL,), f32),
        "rel_bias": 0.2 * jax.random.normal(ks[11], (REL_BUCKETS, N_HEADS_TOTAL), f32),
    }


def reference(x, w_in, w_out_a, w_out_b, w_out, w_ff_gate, w_ff_up, w_ff_down,
              g_mix, g_ffn, g_final, rel_bias):
    B, S, _ = x.shape
    splits = proj_split_points()
    for layer in range(DEPTH):
        h = rms_norm(x, g_mix[layer])
        proj = h @ w_in[layer]
        qa, ka, va, qb, kb, vb, qi, ki, wi, gate_a, gate_b = jnp.split(proj, splits, axis=-1)

        grp = (B, S, N_DIL_GROUPS, HEADS_PER_DIL_GROUP, HEAD_DIM)
        qa, ka, va = qa.reshape(grp), ka.reshape(grp), va.reshape(grp)
        outs, lses = [], []
        for g, (window, dilation) in enumerate(DIL_PATTERNS):
            cols = rel_bias[:, g * HEADS_PER_DIL_GROUP:(g + 1) * HEADS_PER_DIL_GROUP]
            o, lse = dilated_window_attention(qa[:, :, g], ka[:, :, g], va[:, :, g],
                                              dilation, window // dilation, cols)
            outs.append(o)
            lses.append(lse)
        wts = jax.nn.softmax(jnp.stack(lses), axis=0)
        y_a = jnp.sum(wts[..., None] * jnp.stack(outs), axis=0)
        y_a = y_a.reshape(B, S, WIDTH_A_OUT).astype(x.dtype)

        hb = (B, S, N_HEADS_B, HEAD_DIM)
        y_b = dsa_attention(qb.reshape(hb), kb.reshape(hb), vb.reshape(hb),
                            qi.reshape(B, S, N_IDX_HEADS, IDX_DIM), ki, wi,
                            rel_bias[:, N_HEADS_A:]).astype(x.dtype)

        merged = (jax.nn.sigmoid(gate_a) * (y_a @ w_out_a[layer])
                  + jax.nn.sigmoid(gate_b) * (y_b @ w_out_b[layer]))
        x = x + merged @ w_out[layer]

        h = rms_norm(x, g_ffn[layer])
        x = x + (jax.nn.silu(h @ w_ff_gate[layer]) * (h @ w_ff_up[layer])) @ w_ff_down[layer]
    return rms_norm(x, g_final)
```

```python
import functools
import math

import jax
import jax.numpy as jnp
from jax import lax
from jax.experimental import pallas as pl
from jax.experimental.pallas import tpu as pltpu

F32 = jnp.float32
BF16 = jnp.bfloat16

HEAD_DIM = 128
DILATIONS = (1, 4, 16)
BAND = 128
HEADS_PER_GROUP = 4
N_HEADS_A = 12
N_HEADS_B = 8
WIDTH_A = N_HEADS_A * HEAD_DIM
WIDTH_A_OUT = HEADS_PER_GROUP * HEAD_DIM
WIDTH_B = N_HEADS_B * HEAD_DIM
N_IDX_HEADS = 16
IDX_DIM = 64
TOPK_MAX = 256
REL_BUCKETS = 32
REL_MAX_DIST = 2048
RMS_EPS = 1e-6
QKV_WIDTH = 3 * WIDTH_A + 3 * WIDTH_B
IDX_WIDTH = N_IDX_HEADS * IDX_DIM + IDX_DIM + N_IDX_HEADS
IDX_PAD = 1152
KI_COL = N_IDX_HEADS * IDX_DIM
WI_COL = KI_COL + IDX_DIM
LANES = 128
QB = 128
KCHUNK = 4
NEG = -1e30
INT_MIN = -(2 ** 31)
VMEM_LIMIT = 56 * 1024 * 1024


def _params(sem):
    return pltpu.CompilerParams(dimension_semantics=sem, vmem_limit_bytes=VMEM_LIMIT)


def _rms_bf16(x, g):
    y = x * lax.rsqrt(jnp.mean(x * x, axis=-1, keepdims=True) + RMS_EPS)
    return (y * g).astype(BF16)


def _sigmoid(x):
    return 1.0 / (1.0 + jnp.exp(-x))


def _norm_matmul_kernel(x_ref, g_ref, w_ref, o_ref, h_ref, *, head_major):
    @pl.when(pl.program_id(1) == 0)
    def _():
        h_ref[...] = _rms_bf16(x_ref[...], g_ref[...])

    acc = jnp.dot(h_ref[...], w_ref[...], preferred_element_type=F32)
    if head_major:
        for c in range(acc.shape[1] // HEAD_DIM):
            o_ref[0, c] = acc[:, c * HEAD_DIM:(c + 1) * HEAD_DIM].astype(o_ref.dtype)
    else:
        o_ref[...] = acc.astype(o_ref.dtype)


def _norm_matmul(x2, g, w, *, tm, tn, out_dtype, head_major_bs=None, name):
    M, D = x2.shape
    N = w.shape[1]
    grid = (M // tm, N // tn)
    if head_major_bs is None:
        out_shape = jax.ShapeDtypeStruct((M, N), out_dtype)
        out_spec = pl.BlockSpec((tm, tn), lambda i, j: (i, j))
    else:
        B, S = head_major_bs
        per_b = S // tm
        out_shape = jax.ShapeDtypeStruct((B, N // HEAD_DIM, S, HEAD_DIM), out_dtype)
        out_spec = pl.BlockSpec((1, tn // HEAD_DIM, tm, HEAD_DIM),
                                lambda i, j: (i // per_b, j, i % per_b, 0))
    return pl.pallas_call(
        functools.partial(_norm_matmul_kernel, head_major=head_major_bs is not None),
        grid=grid,
        in_specs=[pl.BlockSpec((tm, D), lambda i, j: (i, 0)),
                  pl.BlockSpec((1, D), lambda i, j: (0, 0)),
                  pl.BlockSpec((D, tn), lambda i, j: (0, j))],
        out_specs=out_spec,
        out_shape=out_shape,
        scratch_shapes=[pltpu.VMEM((tm, D), BF16)],
        compiler_params=_params(("parallel", "arbitrary")),
        name=name,
    )(x2, g, w)


def _t5_bucket(dist):
    n = jnp.maximum(dist, 0)
    max_exact = REL_BUCKETS // 2
    nf = jnp.maximum(n, 1).astype(F32)
    large = max_exact + (jnp.log(nf / max_exact) / math.log(REL_MAX_DIST / max_exact)
                         * (REL_BUCKETS - max_exact)).astype(jnp.int32)
    large = jnp.minimum(large, REL_BUCKETS - 1)
    return jnp.where(n < max_exact, n, large)


def _table_lookup(tab_ref, bucket, col):
    val = jnp.zeros(bucket.shape, F32)
    for b in range(REL_BUCKETS):
        val = jnp.where(bucket == b, tab_ref[b, col], val)
    return val


def _band_bias_kernel(tab_ref, o_ref):
    h = pl.program_id(0)
    dilation = jnp.left_shift(1, 2 * (h // HEADS_PER_GROUP))
    qi = lax.broadcasted_iota(jnp.int32, (BAND, 2 * BAND), 0)
    kj = lax.broadcasted_iota(jnp.int32, (BAND, 2 * BAND), 1)
    delta = qi + BAND - kj
    bias = _table_lookup(tab_ref, _t5_bucket(delta * dilation), h)
    o_ref[0] = jnp.where((delta >= 0) & (delta <= BAND), bias, -jnp.inf)


def _toeplitz_bias_kernel(tab_ref, o_ref):
    h = pl.program_id(0)
    qi = lax.broadcasted_iota(jnp.int32, (QB, QB), 0)
    kj = lax.broadcasted_iota(jnp.int32, (QB, QB), 1)
    dist = pl.program_id(1) * QB + qi - kj
    o_ref[0, 0] = _table_lookup(tab_ref, _t5_bucket(dist), N_HEADS_A + h)


def _bias_tiles(rel_bias, n_blocks):
    smem = pl.BlockSpec(memory_space=pltpu.SMEM)
    band = pl.pallas_call(
        _band_bias_kernel,
        grid=(N_HEADS_A,),
        in_specs=[smem],
        out_specs=pl.BlockSpec((1, BAND, 2 * BAND), lambda h: (h, 0, 0)),
        out_shape=jax.ShapeDtypeStruct((N_HEADS_A, BAND, 2 * BAND), F32),
        name="band_bias",
    )(rel_bias)
    toep = pl.pallas_call(
        _toeplitz_bias_kernel,
        grid=(N_HEADS_B, n_blocks),
        in_specs=[smem],
        out_specs=pl.BlockSpec((1, 1, QB, QB), lambda h, d: (h, d, 0, 0)),
        out_shape=jax.ShapeDtypeStruct((N_HEADS_B, n_blocks, QB, QB), F32),
        name="toeplitz_bias",
    )(rel_bias)
    return band, toep


def _dilated_kernel(q_ref, k_ref, v_ref, bias_ref, o_ref, lse_ref, *, n_blk):
    scale = HEAD_DIM ** -0.5
    bias = bias_ref[0]

    def block(row0, kwin, vwin, b):
        q = q_ref[0, 0, pl.ds(row0, BAND), :]
        s = lax.dot_general(q, kwin, (((1,), (1,)), ((), ())), preferred_element_type=F32)
        s = s * scale + b
        m = jnp.max(s, axis=-1, keepdims=True)
        p = jnp.exp(s - m)
        l = jnp.sum(p, axis=-1, keepdims=True)
        o = jnp.dot(p.astype(BF16), vwin, preferred_element_type=F32) / l
        o_ref[0, pl.ds(row0, BAND), :] = o
        lse_ref[0, pl.ds(row0, BAND), :] = jnp.broadcast_to(m + jnp.log(l), (BAND, HEAD_DIM))

    block(0, k_ref[0, 0, 0:BAND, :], v_ref[0, 0, 0:BAND, :], bias[:, BAND:])

    def body(n, carry):
        win0 = pl.multiple_of((n - 1) * BAND, BAND)
        block(pl.multiple_of(n * BAND, BAND),
              k_ref[0, 0, pl.ds(win0, 2 * BAND), :], v_ref[0, 0, pl.ds(win0, 2 * BAND), :], bias)
        return carry

    lax.fori_loop(1, n_blk, body, 0)


def _dilated_attention(qkv, band_bias, group):
    B, _, S, _ = qkv.shape
    d = DILATIONS[group]
    L = S // d
    n_blk = L // BAND
    view = qkv.reshape(B, qkv.shape[1], L, d * HEAD_DIM)
    head0 = group * HEADS_PER_GROUP

    def qkv_spec(base):
        return pl.BlockSpec((1, 1, L, HEAD_DIM), lambda b, h, r: (b, base + head0 + h, 0, r))

    out_spec = pl.BlockSpec((1, L, HEAD_DIM), lambda b, h, r: (b, 0, r * HEADS_PER_GROUP + h))
    out_sds = jax.ShapeDtypeStruct((B, L, d * WIDTH_A_OUT), F32)
    o, lse = pl.pallas_call(
        functools.partial(_dilated_kernel, n_blk=n_blk),
        grid=(B, HEADS_PER_GROUP, d),
        in_specs=[qkv_spec(0), qkv_spec(N_HEADS_A), qkv_spec(2 * N_HEADS_A),
                  pl.BlockSpec((1, BAND, 2 * BAND), lambda b, h, r: (head0 + h, 0, 0))],
        out_specs=[out_spec, out_spec],
        out_shape=[out_sds, out_sds],
        compiler_params=_params(("parallel", "parallel", "parallel")),
        name=f"dilated_attn_d{d}",
    )(view, view, view, band_bias)
    return o.reshape(B, S, WIDTH_A_OUT), lse.reshape(B, S, WIDTH_A_OUT)


def _indexer_kernel(q_ref, k_ref, mask_ref, qh_ref, wb_ref, key_ref, *, n_kb, topk):
    i = pl.program_id(1)
    n_vis = i + 1
    for h in range(N_IDX_HEADS):
        qh_ref[h * QB:(h + 1) * QB, :] = q_ref[0, :, h * IDX_DIM:(h + 1) * IDX_DIM].astype(BF16)
        w = q_ref[0, :, WI_COL + h:WI_COL + h + 1] * (N_IDX_HEADS ** -0.5) * (IDX_DIM ** -0.5)
        wb_ref[h] = jnp.broadcast_to(w, (QB, LANES))

    qi = lax.broadcasted_iota(jnp.int32, (QB, QB), 0)
    kj = lax.broadcasted_iota(jnp.int32, (QB, QB), 1)
    tril = kj <= qi

    def score_body(kb, carry):
        kblk = k_ref[0, pl.ds(pl.multiple_of(kb * QB, QB), QB), 0:IDX_DIM].astype(BF16)
        d = lax.dot_general(qh_ref[...], kblk, (((1,), (1,)), ((), ())),
                            preferred_element_type=F32)
        acc = jnp.zeros((QB, QB), F32)
        for h in range(N_IDX_HEADS):
            acc = acc + jnp.maximum(d[h * QB:(h + 1) * QB], 0.0) * wb_ref[h]
        acc = jnp.where(tril | (kb < i), acc, -jnp.inf)
        bits = pltpu.bitcast(acc, jnp.int32)
        key_ref[kb] = bits ^ ((bits >> 31) & 0x7FFFFFFF)
        return carry

    lax.fori_loop(0, n_vis, score_body, 0)

    def bit_body(it, t):
        cand = t + jnp.left_shift(1, 31 - it)
        cand_b = jnp.broadcast_to(cand, (QB, LANES))

        def count_body(kb, cnt):
            return cnt + (key_ref[kb] >= cand_b).astype(jnp.int32)

        cnt = lax.fori_loop(0, n_vis, count_body, jnp.zeros((QB, LANES), jnp.int32))
        total = jnp.sum(cnt, axis=-1, keepdims=True)
        return jnp.where(total >= topk, cand, t)

    thr = lax.fori_loop(0, 32, bit_body, jnp.full((QB, 1), INT_MIN, jnp.int32))
    thr_b = jnp.broadcast_to(thr, (QB, LANES))

    def mask_body(kb, carry):
        sel = (key_ref[kb] >= thr_b) & (tril | (kb < i))
        mask_ref[0, 0, kb] = sel.astype(jnp.int8)
        return carry

    lax.fori_loop(0, n_vis, mask_body, 0)

    def zero_body(kb, carry):
        mask_ref[0, 0, kb] = jnp.zeros((QB, QB), jnp.int8)
        return carry

    lax.fori_loop(n_vis, n_kb, zero_body, 0)


def _indexer_mask(idx, topk):
    B, S, _ = idx.shape
    n_kb = S // QB
    return pl.pallas_call(
        functools.partial(_indexer_kernel, n_kb=n_kb, topk=topk),
        grid=(B, n_kb),
        in_specs=[pl.BlockSpec((1, QB, IDX_PAD), lambda b, i: (b, i, 0)),
                  pl.BlockSpec((1, S, LANES), lambda b, i: (b, 0, KI_COL // LANES))],
        out_specs=pl.BlockSpec((1, 1, n_kb, QB, QB), lambda b, i: (b, i, 0, 0, 0)),
        out_shape=jax.ShapeDtypeStruct((B, n_kb, n_kb, QB, QB), jnp.int8),
        scratch_shapes=[pltpu.VMEM((N_IDX_HEADS * QB, IDX_DIM), BF16),
                        pltpu.VMEM((N_IDX_HEADS, QB, LANES), F32),
                        pltpu.VMEM((n_kb, QB, QB), jnp.int32)],
        compiler_params=_params(("parallel", "parallel")),
        name="indexer_topk_mask",
    )(idx, idx)


def _sparse_attn_kernel(q_ref, k_ref, v_ref, bias_ref, mask_ref, o_ref):
    i = pl.program_id(2)
    scale = HEAD_DIM ** -0.5
    q = q_ref[0, 0]
    width = KCHUNK * QB

    def body(c, carry):
        m, l, acc = carry
        row0 = pl.multiple_of(c * width, width)
        s = lax.dot_general(q, k_ref[0, 0, pl.ds(row0, width), :], (((1,), (1,)), ((), ())),
                            preferred_element_type=F32) * scale
        parts = []
        for t in range(KCHUNK):
            kb = c * KCHUNK + t
            bias = bias_ref[0, jnp.maximum(i - kb, 0)]
            keep = mask_ref[0, 0, kb].astype(jnp.int32) != 0
            parts.append(jnp.where(keep, s[:, t * QB:(t + 1) * QB] + bias, NEG))
        s = jnp.concatenate(parts, axis=1)
        m_new = jnp.maximum(m, jnp.max(s, axis=-1, keepdims=True))
        alpha = jnp.exp(m - m_new)
        p = jnp.exp(s - m_new)
        l = alpha * l + jnp.sum(p, axis=-1, keepdims=True)
        acc = alpha * acc + jnp.dot(p.astype(BF16), v_ref[0, 0, pl.ds(row0, width), :],
                                    preferred_element_type=F32)
        return m_new, l, acc

    init = (jnp.full((QB, 1), NEG, F32), jnp.zeros((QB, 1), F32), jnp.zeros((QB, HEAD_DIM), F32))
    _, l, acc = lax.fori_loop(0, i // KCHUNK + 1, body, init)
    o_ref[0] = (acc / l).astype(o_ref.dtype)


def _sparse_attention(qkv, toep_bias, mask):
    B, _, S, _ = qkv.shape
    n_kb = S // QB
    q0 = 3 * N_HEADS_A

    def kv_spec(base):
        return pl.BlockSpec((1, 1, S, HEAD_DIM), lambda b, h, i: (b, base + h, 0, 0))

    return pl.pallas_call(
        _sparse_attn_kernel,
        grid=(B, N_HEADS_B, n_kb),
        in_specs=[pl.BlockSpec((1, 1, QB, HEAD_DIM), lambda b, h, i: (b, q0 + h, i, 0)),
                  kv_spec(q0 + N_HEADS_B), kv_spec(q0 + 2 * N_HEADS_B),
                  pl.BlockSpec((1, n_kb, QB, QB), lambda b, h, i: (h, 0, 0, 0)),
                  pl.BlockSpec((1, 1, n_kb, QB, QB), lambda b, h, i: (b, i, 0, 0, 0))],
        out_specs=pl.BlockSpec((1, QB, HEAD_DIM), lambda b, h, i: (b, i, h)),
        out_shape=jax.ShapeDtypeStruct((B, S, WIDTH_B), BF16),
        compiler_params=_params(("parallel", "parallel", "parallel")),
        name="sparse_attn",
    )(qkv, qkv, qkv, toep_bias, mask)


def _merge_kernel(o1_ref, o2_ref, o3_ref, l1_ref, l2_ref, l3_ref, yb_ref, ga_ref, gb_ref,
                  wa_ref, wb_ref, out_ref, ya_ref):
    @pl.when(pl.program_id(1) == 0)
    def _():
        l1, l2, l3 = l1_ref[...], l2_ref[...], l3_ref[...]
        mx = jnp.maximum(jnp.maximum(l1, l2), l3)
        e1, e2, e3 = jnp.exp(l1 - mx), jnp.exp(l2 - mx), jnp.exp(l3 - mx)
        tot = e1 + e2 + e3
        ya = (e1 / tot) * o1_ref[...] + (e2 / tot) * o2_ref[...] + (e3 / tot) * o3_ref[...]
        ya_ref[...] = ya.astype(BF16)

    a = jnp.dot(ya_ref[...], wa_ref[...], preferred_element_type=F32)
    b = jnp.dot(yb_ref[...], wb_ref[...], preferred_element_type=F32)
    out_ref[...] = (_sigmoid(ga_ref[...]) * a + _sigmoid(gb_ref[...]) * b).astype(out_ref.dtype)


def _merge(os_, lses, yb, gates, wa, wb, *, tm, tn):
    M = yb.shape[0]
    N = wa.shape[1]
    row = lambda w: pl.BlockSpec((tm, w), lambda i, j: (i, 0))
    return pl.pallas_call(
        _merge_kernel,
        grid=(M // tm, N // tn),
        in_specs=[row(WIDTH_A_OUT)] * 6 + [row(WIDTH_B),
                  pl.BlockSpec((tm, tn), lambda i, j: (i, j)),
                  pl.BlockSpec((tm, tn), lambda i, j: (i, j + N // tn)),
                  pl.BlockSpec((WIDTH_A_OUT, tn), lambda i, j: (0, j)),
                  pl.BlockSpec((WIDTH_B, tn), lambda i, j: (0, j))],
        out_specs=pl.BlockSpec((tm, tn), lambda i, j: (i, j)),
        out_shape=jax.ShapeDtypeStruct((M, N), BF16),
        scratch_shapes=[pltpu.VMEM((tm, WIDTH_A_OUT), BF16)],
        compiler_params=_params(("parallel", "arbitrary")),
        name="gated_merge",
    )(*os_, *lses, yb, gates, gates, wa, wb)


def _matmul_residual_kernel(a_ref, w_ref, r_ref, o_ref, acc_ref):
    k = pl.program_id(2)

    @pl.when(k == 0)
    def _():
        acc_ref[...] = jnp.zeros_like(acc_ref)

    acc_ref[...] += jnp.dot(a_ref[...], w_ref[...], preferred_element_type=F32)

    @pl.when(k == pl.num_programs(2) - 1)
    def _():
        o_ref[...] = r_ref[...] + acc_ref[...]


def _matmul_residual(a, w, r, *, tm, tn, tk, name):
    M, K = a.shape
    N = w.shape[1]
    return pl.pallas_call(
        _matmul_residual_kernel,
        grid=(M // tm, N // tn, K // tk),
        in_specs=[pl.BlockSpec((tm, tk), lambda i, j, k: (i, k)),
                  pl.BlockSpec((tk, tn), lambda i, j, k: (k, j)),
                  pl.BlockSpec((tm, tn), lambda i, j, k: (i, j))],
        out_specs=pl.BlockSpec((tm, tn), lambda i, j, k: (i, j)),
        out_shape=jax.ShapeDtypeStruct((M, N), F32),
        scratch_shapes=[pltpu.VMEM((tm, tn), F32)],
        compiler_params=_params(("parallel", "parallel", "arbitrary")),
        name=name,
    )(a, w, r)


def _swiglu_kernel(x_ref, g_ref, wg_ref, wu_ref, o_ref, h_ref):
    @pl.when(pl.program_id(1) == 0)
    def _():
        h_ref[...] = _rms_bf16(x_ref[...], g_ref[...])

    h = h_ref[...]
    gate = jnp.dot(h, wg_ref[...], preferred_element_type=F32)
    up = jnp.dot(h, wu_ref[...], preferred_element_type=F32)
    o_ref[...] = (gate * _sigmoid(gate) * up).astype(o_ref.dtype)


def _swiglu(x2, g, wg, wu, *, tm, tn):
    M, D = x2.shape
    N = wg.shape[1]
    wspec = pl.BlockSpec((D, tn), lambda i, j: (0, j))
    return pl.pallas_call(
        _swiglu_kernel,
        grid=(M // tm, N // tn),
        in_specs=[pl.BlockSpec((tm, D), lambda i, j: (i, 0)),
                  pl.BlockSpec((1, D), lambda i, j: (0, 0)), wspec, wspec],
        out_specs=pl.BlockSpec((tm, tn), lambda i, j: (i, j)),
        out_shape=jax.ShapeDtypeStruct((M, N), BF16),
        scratch_shapes=[pltpu.VMEM((tm, D), BF16)],
        compiler_params=_params(("parallel", "arbitrary")),
        name="swiglu_gate_up",
    )(x2, g, wg, wu)


def _final_norm_kernel(x_ref, g_ref, o_ref):
    x = x_ref[...]
    y = x * lax.rsqrt(jnp.mean(x * x, axis=-1, keepdims=True) + RMS_EPS)
    o_ref[...] = y * g_ref[...]


def _final_norm(x2, g, *, tm):
    M, D = x2.shape
    return pl.pallas_call(
        _final_norm_kernel,
        grid=(M // tm,),
        in_specs=[pl.BlockSpec((tm, D), lambda i: (i, 0)), pl.BlockSpec((1, D), lambda i: (0, 0))],
        out_specs=pl.BlockSpec((tm, D), lambda i: (i, 0)),
        out_shape=jax.ShapeDtypeStruct((M, D), F32),
        compiler_params=_params(("parallel",)),
        name="final_rmsnorm",
    )(x2, g)


def kernel(x, w_in, w_out_a, w_out_b, w_out, w_ff_gate, w_ff_up, w_ff_down,
           g_mix, g_ffn, g_final, rel_bias):
    B, S, D = x.shape
    depth = w_in.shape[0]
    d_ff = w_ff_gate.shape[2]
    M = B * S
    assert S % (BAND * max(DILATIONS)) == 0 and S % (KCHUNK * QB) == 0
    assert w_in.shape[2] == QKV_WIDTH + IDX_WIDTH + 2 * D
    topk = min(TOPK_MAX, S // 4)
    tm = 1024

    band_bias, toep_bias = _bias_tiles(rel_bias, S // QB)
    x2 = x.reshape(M, D)
    for layer in range(depth):
        wl = w_in[layer]
        w_qkv = wl[:, :QKV_WIDTH].astype(BF16)
        w_idx = jnp.pad(wl[:, QKV_WIDTH:QKV_WIDTH + IDX_WIDTH],
                        ((0, 0), (0, IDX_PAD - IDX_WIDTH))).astype(BF16)
        w_gate = wl[:, QKV_WIDTH + IDX_WIDTH:].astype(BF16)
        g = g_mix[layer].reshape(1, D)

        qkv = _norm_matmul(x2, g, w_qkv, tm=tm, tn=512, out_dtype=BF16,
                           head_major_bs=(B, S), name="proj_qkv")
        idx = _norm_matmul(x2, g, w_idx, tm=tm, tn=IDX_PAD, out_dtype=F32, name="proj_idx")
        gates = _norm_matmul(x2, g, w_gate, tm=tm, tn=512, out_dtype=F32, name="proj_gates")

        outs, lses = [], []
        for group in range(len(DILATIONS)):
            o, lse = _dilated_attention(qkv, band_bias, group)
            outs.append(o.reshape(M, WIDTH_A_OUT))
            lses.append(lse.reshape(M, WIDTH_A_OUT))

        mask = _indexer_mask(idx.reshape(B, S, IDX_PAD), topk)
        y_b = _sparse_attention(qkv, toep_bias, mask).reshape(M, WIDTH_B)

        merged = _merge(outs, lses, y_b, gates, w_out_a[layer].astype(BF16),
                        w_out_b[layer].astype(BF16), tm=tm, tn=512)
        x2 = _matmul_residual(merged, w_out[layer].astype(BF16), x2,
                              tm=tm, tn=512, tk=D, name="out_proj")

        act = _swiglu(x2, g_ffn[layer].reshape(1, D), w_ff_gate[layer].astype(BF16),
                      w_ff_up[layer].astype(BF16), tm=tm, tn=512)
        x2 = _matmul_residual(act, w_ff_down[layer].astype(BF16), x2,
                              tm=tm, tn=512, tk=d_ff // 2, name="ffn_down")
    return _final_norm(x2, g_final.reshape(1, D), tm=512).reshape(B, S, D)
```

```python
import functools
import math

import jax
import jax.numpy as jnp
from jax import lax
from jax.experimental import pallas as pl
from jax.experimental.pallas import tpu as pltpu

F32 = jnp.float32
BF16 = jnp.bfloat16

HEAD_DIM = 128
DILATIONS = (1, 4, 16)
BAND = 128
HEADS_PER_GROUP = 4
N_HEADS_A = 12
N_HEADS_B = 8
WIDTH_A = N_HEADS_A * HEAD_DIM
WIDTH_A_OUT = HEADS_PER_GROUP * HEAD_DIM
WIDTH_B = N_HEADS_B * HEAD_DIM
N_IDX_HEADS = 16
IDX_DIM = 64
TOPK_MAX = 256
REL_BUCKETS = 32
REL_MAX_DIST = 2048
RMS_EPS = 1e-6
QKV_WIDTH = 3 * WIDTH_A + 3 * WIDTH_B
IDX_WIDTH = N_IDX_HEADS * IDX_DIM + IDX_DIM + N_IDX_HEADS
IDX_PAD = 1152
KI_COL = N_IDX_HEADS * IDX_DIM
WI_COL = KI_COL + IDX_DIM
LANES = 128
QB = 128
KCHUNK = 4
QT = KCHUNK * QB
NEG = -1e30
INT_MIN = -(2 ** 31)
VMEM_LIMIT = 56 * 1024 * 1024


def _params(sem):
    return pltpu.CompilerParams(dimension_semantics=sem, vmem_limit_bytes=VMEM_LIMIT)


def _rms_bf16(x, g):
    y = x * lax.rsqrt(jnp.mean(x * x, axis=-1, keepdims=True) + RMS_EPS)
    return (y * g).astype(BF16)


def _sigmoid(x):
    return 1.0 / (1.0 + jnp.exp(-x))


def _norm_matmul_kernel(x_ref, g_ref, w_ref, o_ref, h_ref, *, layout):
    @pl.when(pl.program_id(1) == 0)
    def _():
        h_ref[...] = _rms_bf16(x_ref[...], g_ref[...])

    acc = jnp.dot(h_ref[...], w_ref[...], preferred_element_type=F32)
    if layout == "rows":
        o_ref[...] = acc.astype(o_ref.dtype)
    else:
        for c in range(acc.shape[1] // HEAD_DIM):
            head = acc[:, c * HEAD_DIM:(c + 1) * HEAD_DIM]
            if layout == "head":
                o_ref[0, c] = head.astype(o_ref.dtype)
            else:
                for r in range(acc.shape[0] // QT):
                    o_ref[0, c, r] = head[r * QT:(r + 1) * QT].T.astype(o_ref.dtype)


def _norm_matmul(x2, g, w, *, tm, tn, out_dtype, layout="rows", batch_seq=None, name):
    M, D = x2.shape
    N = w.shape[1]
    grid = (M // tm, N // tn)
    if layout == "rows":
        out_shape = jax.ShapeDtypeStruct((M, N), out_dtype)
        out_spec = pl.BlockSpec((tm, tn), lambda i, j: (i, j))
    else:
        B, S = batch_seq
        per_b = S // tm
        if layout == "head":
            out_shape = jax.ShapeDtypeStruct((B, N // HEAD_DIM, S, HEAD_DIM), out_dtype)
            out_spec = pl.BlockSpec((1, tn // HEAD_DIM, tm, HEAD_DIM),
                                    lambda i, j: (i // per_b, j, i % per_b, 0))
        else:
            out_shape = jax.ShapeDtypeStruct((B, N // HEAD_DIM, S // QT, HEAD_DIM, QT), out_dtype)
            out_spec = pl.BlockSpec((1, tn // HEAD_DIM, tm // QT, HEAD_DIM, QT),
                                    lambda i, j: (i // per_b, j, i % per_b, 0, 0))
    return pl.pallas_call(
        functools.partial(_norm_matmul_kernel, layout=layout),
        grid=grid,
        in_specs=[pl.BlockSpec((tm, D), lambda i, j: (i, 0)),
                  pl.BlockSpec((1, D), lambda i, j: (0, 0)),
                  pl.BlockSpec((D, tn), lambda i, j: (0, j))],
        out_specs=out_spec,
        out_shape=out_shape,
        scratch_shapes=[pltpu.VMEM((tm, D), BF16)],
        compiler_params=_params(("parallel", "arbitrary")),
        name=name,
    )(x2, g, w)


def _t5_bucket(dist):
    n = jnp.maximum(dist, 0)
    max_exact = REL_BUCKETS // 2
    nf = jnp.maximum(n, 1).astype(F32)
    large = max_exact + (jnp.log(nf / max_exact) / math.log(REL_MAX_DIST / max_exact)
                         * (REL_BUCKETS - max_exact)).astype(jnp.int32)
    large = jnp.minimum(large, REL_BUCKETS - 1)
    return jnp.where(n < max_exact, n, large)


def _table_lookup(tab_ref, bucket, col):
    val = jnp.zeros(bucket.shape, F32)
    for b in range(REL_BUCKETS):
        val = jnp.where(bucket == b, tab_ref[b, col], val)
    return val


def _band_bias_kernel(tab_ref, o_ref):
    h = pl.program_id(0)
    dilation = jnp.left_shift(1, 2 * (h // HEADS_PER_GROUP))
    qi = lax.broadcasted_iota(jnp.int32, (BAND, 2 * BAND), 0)
    kj = lax.broadcasted_iota(jnp.int32, (BAND, 2 * BAND), 1)
    delta = qi + BAND - kj
    bias = _table_lookup(tab_ref, _t5_bucket(delta * dilation), h)
    o_ref[0] = jnp.where((delta >= 0) & (delta <= BAND), bias, -jnp.inf)


def _toeplitz_bias_kernel(tab_ref, o_ref):
    h = pl.program_id(0)
    kj = lax.broadcasted_iota(jnp.int32, (QB, QB), 0)
    qi = lax.broadcasted_iota(jnp.int32, (QB, QB), 1)
    dist = pl.program_id(1) * QB + qi - kj
    o_ref[0, 0] = _table_lookup(tab_ref, _t5_bucket(dist), N_HEADS_A + h)


def _bias_tiles(rel_bias, n_blocks):
    smem = pl.BlockSpec(memory_space=pltpu.SMEM)
    band = pl.pallas_call(
        _band_bias_kernel,
        grid=(N_HEADS_A,),
        in_specs=[smem],
        out_specs=pl.BlockSpec((1, BAND, 2 * BAND), lambda h: (h, 0, 0)),
        out_shape=jax.ShapeDtypeStruct((N_HEADS_A, BAND, 2 * BAND), F32),
        name="band_bias",
    )(rel_bias)
    toep = pl.pallas_call(
        _toeplitz_bias_kernel,
        grid=(N_HEADS_B, n_blocks),
        in_specs=[smem],
        out_specs=pl.BlockSpec((1, 1, QB, QB), lambda h, d: (h, d, 0, 0)),
        out_shape=jax.ShapeDtypeStruct((N_HEADS_B, n_blocks, QB, QB), F32),
        name="toeplitz_bias",
    )(rel_bias)
    return band, toep


def _dilated_kernel(q_ref, k_ref, v_ref, bias_ref, o_ref, lse_ref, *, n_blk):
    scale = HEAD_DIM ** -0.5
    bias = bias_ref[0]

    def block(row0, kwin, vwin, b):
        q = q_ref[0, 0, pl.ds(row0, BAND), :]
        s = lax.dot_general(q, kwin, (((1,), (1,)), ((), ())), preferred_element_type=F32)
        s = s * scale + b
        m = jnp.max(s, axis=-1, keepdims=True)
        p = jnp.exp(s - m)
        l = jnp.sum(p, axis=-1, keepdims=True)
        o = jnp.dot(p.astype(BF16), vwin, preferred_element_type=F32) / l
        o_ref[0, pl.ds(row0, BAND), :] = o
        lse_ref[0, pl.ds(row0, BAND), :] = jnp.broadcast_to(m + jnp.log(l), (BAND, HEAD_DIM))

    block(0, k_ref[0, 0, 0:BAND, :], v_ref[0, 0, 0:BAND, :], bias[:, BAND:])

    def body(n, carry):
        win0 = pl.multiple_of((n - 1) * BAND, BAND)
        block(pl.multiple_of(n * BAND, BAND),
              k_ref[0, 0, pl.ds(win0, 2 * BAND), :], v_ref[0, 0, pl.ds(win0, 2 * BAND), :], bias)
        return carry

    lax.fori_loop(1, n_blk, body, 0)


def _dilated_attention(qkv, band_bias, group):
    B, _, S, _ = qkv.shape
    d = DILATIONS[group]
    L = S // d
    n_blk = L // BAND
    view = qkv.reshape(B, qkv.shape[1], L, d * HEAD_DIM)
    head0 = group * HEADS_PER_GROUP

    def qkv_spec(base):
        return pl.BlockSpec((1, 1, L, HEAD_DIM), lambda b, h, r: (b, base + head0 + h, 0, r))

    out_spec = pl.BlockSpec((1, L, HEAD_DIM), lambda b, h, r: (b, 0, r * HEADS_PER_GROUP + h))
    out_sds = jax.ShapeDtypeStruct((B, L, d * WIDTH_A_OUT), F32)
    o, lse = pl.pallas_call(
        functools.partial(_dilated_kernel, n_blk=n_blk),
        grid=(B, HEADS_PER_GROUP, d),
        in_specs=[qkv_spec(0), qkv_spec(N_HEADS_A), qkv_spec(2 * N_HEADS_A),
                  pl.BlockSpec((1, BAND, 2 * BAND), lambda b, h, r: (head0 + h, 0, 0))],
        out_specs=[out_spec, out_spec],
        out_shape=[out_sds, out_sds],
        compiler_params=_params(("parallel", "parallel", "parallel")),
        name=f"dilated_attn_d{d}",
    )(view, view, view, band_bias)
    return o.reshape(B, S, WIDTH_A_OUT), lse.reshape(B, S, WIDTH_A_OUT)


def _indexer_kernel(q_ref, k_ref, mask_ref, qh_ref, wb_ref, key_ref, *, n_kb, topk):
    i = pl.program_id(1)
    n_vis = i + 1
    n_chunks = (i + KCHUNK) // KCHUNK
    sub = QB // 8
    for h in range(N_IDX_HEADS):
        qh_ref[h * QB:(h + 1) * QB, :] = q_ref[0, :, h * IDX_DIM:(h + 1) * IDX_DIM].astype(BF16)
    wt = q_ref[0, :, KI_COL:KI_COL + LANES].T * ((N_IDX_HEADS ** -0.5) * (IDX_DIM ** -0.5))
    for h in range(N_IDX_HEADS):
        wb_ref[h] = jnp.broadcast_to(wt[IDX_DIM + h:IDX_DIM + h + 1, :], (8, QB))

    key_pos = (lax.broadcasted_iota(jnp.int32, (sub, 8, QB), 0) * 8
               + lax.broadcasted_iota(jnp.int32, (sub, 8, QB), 1))
    diag_vis = key_pos <= lax.broadcasted_iota(jnp.int32, (sub, 8, QB), 2)

    def score_body(c, carry):
        row0 = pl.multiple_of(c * QT, QT)
        kchunk = k_ref[0, pl.ds(row0, QT), 0:IDX_DIM].astype(BF16)
        accs = [jnp.zeros((sub, 8, QB), F32) for _ in range(KCHUNK)]
        for pair in range(N_IDX_HEADS // 2):
            d = lax.dot_general(kchunk, qh_ref[pair * 2 * QB:(pair + 1) * 2 * QB, :],
                                (((1,), (1,)), ((), ())), preferred_element_type=F32)
            for u in range(2):
                w = wb_ref[2 * pair + u][None]
                for t in range(KCHUNK):
                    tile = d[t * QB:(t + 1) * QB, u * QB:(u + 1) * QB].reshape(sub, 8, QB)
                    accs[t] = accs[t] + jnp.maximum(tile, 0.0) * w
        for t in range(KCHUNK):
            kb = c * KCHUNK + t
            vis = (kb < i) | ((kb == i) & diag_vis)
            bits = pltpu.bitcast(jnp.where(vis, accs[t], -jnp.inf), jnp.int32)
            key_ref[pl.ds(pl.multiple_of(kb * sub, sub), sub)] = bits ^ ((bits >> 31) & 0x7FFFFFFF)
        return carry

    lax.fori_loop(0, n_chunks, score_body, 0)

    def bit_body(it, t):
        cand = t + jnp.left_shift(1, 31 - it)
        cand_b = jnp.broadcast_to(cand, (8, QB))[None]

        def count_body(c, cnt):
            for u in range(KCHUNK):
                keys = key_ref[pl.ds(pl.multiple_of((c * KCHUNK + u) * sub, sub), sub)]
                cnt = cnt + jnp.sum((keys >= cand_b).astype(jnp.int32), axis=0)
            return cnt

        cnt = lax.fori_loop(0, n_chunks, count_body, jnp.zeros((8, QB), jnp.int32))
        total = jnp.sum(cnt, axis=0, keepdims=True)
        return jnp.where(total >= topk, cand, t)

    thr = lax.fori_loop(0, 32, bit_body, jnp.full((1, QB), INT_MIN, jnp.int32))
    thr_b = jnp.broadcast_to(thr, (8, QB))[None]

    def mask_body(kb, carry):
        keys = key_ref[pl.ds(pl.multiple_of(kb * sub, sub), sub)]
        sel = (keys >= thr_b) & ((kb < i) | diag_vis)
        mask_ref[0, 0, kb] = sel.astype(jnp.int32).reshape(QB, QB).astype(jnp.int8)
        return carry

    lax.fori_loop(0, n_vis, mask_body, 0)

    def zero_body(kb, carry):
        mask_ref[0, 0, kb] = jnp.zeros((QB, QB), jnp.int8)
        return carry

    lax.fori_loop(n_vis, n_kb, zero_body, 0)


def _indexer_mask(idx, topk):
    B, S, _ = idx.shape
    n_kb = S // QB
    return pl.pallas_call(
        functools.partial(_indexer_kernel, n_kb=n_kb, topk=topk),
        grid=(B, n_kb),
        in_specs=[pl.BlockSpec((1, QB, IDX_PAD), lambda b, i: (b, i, 0)),
                  pl.BlockSpec((1, S, LANES), lambda b, i: (b, 0, KI_COL // LANES))],
        out_specs=pl.BlockSpec((1, 1, n_kb, QB, QB), lambda b, i: (b, i, 0, 0, 0)),
        out_shape=jax.ShapeDtypeStruct((B, n_kb, n_kb, QB, QB), jnp.int8),
        scratch_shapes=[pltpu.VMEM((N_IDX_HEADS * QB, IDX_DIM), BF16),
                        pltpu.VMEM((N_IDX_HEADS, 8, QB), F32),
                        pltpu.VMEM((n_kb * QB // 8, 8, QB), jnp.int32)],
        compiler_params=_params(("parallel", "parallel")),
        name="indexer_topk_mask",
    )(idx, idx)


def _sparse_attn_kernel(q_ref, k_ref, vt_ref, bias_ref, mask_ref, o_ref, m_ref, l_ref, acc_ref):
    i = pl.program_id(2)
    scale = HEAD_DIM ** -0.5
    q = q_ref[0, 0]
    m_ref[...] = jnp.full(m_ref.shape, NEG, F32)
    l_ref[...] = jnp.zeros(l_ref.shape, F32)
    acc_ref[...] = jnp.zeros(acc_ref.shape, F32)

    def body(c, carry):
        row0 = pl.multiple_of(c * QT, QT)
        s = lax.dot_general(k_ref[0, 0, pl.ds(row0, QT), :], q, (((1,), (1,)), ((), ())),
                            preferred_element_type=F32) * scale
        rows = []
        for t in range(KCHUNK):
            parts = []
            for a in range(KCHUNK):
                bias = bias_ref[0, jnp.maximum((i - c) * KCHUNK + (a - t), 0)]
                keep = mask_ref[0, a, c * KCHUNK + t].astype(jnp.int32) != 0
                tile = s[t * QB:(t + 1) * QB, a * QB:(a + 1) * QB]
                parts.append(jnp.where(keep, tile + bias, NEG))
            rows.append(jnp.concatenate(parts, axis=1))
        s = jnp.concatenate(rows, axis=0)
        m_old = m_ref[...]
        m_new = jnp.maximum(m_old, jnp.max(s, axis=0, keepdims=True))
        alpha = jnp.exp(m_old - m_new)
        p = jnp.exp(s - m_new)
        m_ref[...] = m_new
        l_ref[...] = alpha * l_ref[...] + jnp.sum(p, axis=0, keepdims=True)
        acc_ref[...] = alpha * acc_ref[...] + jnp.dot(
            vt_ref[0, 0, c], p.astype(BF16), preferred_element_type=F32)
        return carry

    lax.fori_loop(0, i + 1, body, 0)
    o_ref[0] = (acc_ref[...] / l_ref[...]).T.astype(o_ref.dtype)


def _sparse_attention(qkv, vt, toep_bias, mask):
    B, _, S, _ = qkv.shape
    n_kb = S // QB
    q0 = 3 * N_HEADS_A

    return pl.pallas_call(
        _sparse_attn_kernel,
        grid=(B, N_HEADS_B, S // QT),
        in_specs=[pl.BlockSpec((1, 1, QT, HEAD_DIM), lambda b, h, i: (b, q0 + h, i, 0)),
                  pl.BlockSpec((1, 1, S, HEAD_DIM), lambda b, h, i: (b, q0 + N_HEADS_B + h, 0, 0)),
                  pl.BlockSpec((1, 1, S // QT, HEAD_DIM, QT), lambda b, h, i: (b, h, 0, 0, 0)),
                  pl.BlockSpec((1, n_kb, QB, QB), lambda b, h, i: (h, 0, 0, 0)),
                  pl.BlockSpec((1, KCHUNK, n_kb, QB, QB), lambda b, h, i: (b, i, 0, 0, 0))],
        out_specs=pl.BlockSpec((1, QT, HEAD_DIM), lambda b, h, i: (b, i, h)),
        out_shape=jax.ShapeDtypeStruct((B, S, WIDTH_B), BF16),
        scratch_shapes=[pltpu.VMEM((1, QT), F32), pltpu.VMEM((1, QT), F32),
                        pltpu.VMEM((HEAD_DIM, QT), F32)],
        compiler_params=_params(("parallel", "parallel", "parallel")),
        name="sparse_attn",
    )(qkv, qkv, vt, toep_bias, mask)


def _merge_kernel(o1_ref, o2_ref, o3_ref, l1_ref, l2_ref, l3_ref, yb_ref, ga_ref, gb_ref,
                  wa_ref, wb_ref, out_ref, ya_ref):
    @pl.when(pl.program_id(1) == 0)
    def _():
        l1, l2, l3 = l1_ref[...], l2_ref[...], l3_ref[...]
        mx = jnp.maximum(jnp.maximum(l1, l2), l3)
        e1, e2, e3 = jnp.exp(l1 - mx), jnp.exp(l2 - mx), jnp.exp(l3 - mx)
        tot = e1 + e2 + e3
        ya = (e1 / tot) * o1_ref[...] + (e2 / tot) * o2_ref[...] + (e3 / tot) * o3_ref[...]
        ya_ref[...] = ya.astype(BF16)

    a = jnp.dot(ya_ref[...], wa_ref[...], preferred_element_type=F32)
    b = jnp.dot(yb_ref[...], wb_ref[...], preferred_element_type=F32)
    out_ref[...] = (_sigmoid(ga_ref[...]) * a + _sigmoid(gb_ref[...]) * b).astype(out_ref.dtype)


def _merge(os_, lses, yb, gates, wa, wb, *, tm, tn):
    M = yb.shape[0]
    N = wa.shape[1]
    row = lambda w: pl.BlockSpec((tm, w), lambda i, j: (i, 0))
    return pl.pallas_call(
        _merge_kernel,
        grid=(M // tm, N // tn),
        in_specs=[row(WIDTH_A_OUT)] * 6 + [row(WIDTH_B),
                  pl.BlockSpec((tm, tn), lambda i, j: (i, j)),
                  pl.BlockSpec((tm, tn), lambda i, j: (i, j + N // tn)),
                  pl.BlockSpec((WIDTH_A_OUT, tn), lambda i, j: (0, j)),
                  pl.BlockSpec((WIDTH_B, tn), lambda i, j: (0, j))],
        out_specs=pl.BlockSpec((tm, tn), lambda i, j: (i, j)),
        out_shape=jax.ShapeDtypeStruct((M, N), BF16),
        scratch_shapes=[pltpu.VMEM((tm, WIDTH_A_OUT), BF16)],
        compiler_params=_params(("parallel", "arbitrary")),
        name="gated_merge",
    )(*os_, *lses, yb, gates, gates, wa, wb)


def _matmul_residual_kernel(a_ref, w_ref, r_ref, o_ref, acc_ref):
    k = pl.program_id(2)

    @pl.when(k == 0)
    def _():
        acc_ref[...] = jnp.zeros_like(acc_ref)

    acc_ref[...] += jnp.dot(a_ref[...], w_ref[...], preferred_element_type=F32)

    @pl.when(k == pl.num_programs(2) - 1)
    def _():
        o_ref[...] = r_ref[...] + acc_ref[...]


def _matmul_residual(a, w, r, *, tm, tn, tk, name):
    M, K = a.shape
    N = w.shape[1]
    return pl.pallas_call(
        _matmul_residual_kernel,
        grid=(M // tm, N // tn, K // tk),
        in_specs=[pl.BlockSpec((tm, tk), lambda i, j, k: (i, k)),
                  pl.BlockSpec((tk, tn), lambda i, j, k: (k, j)),
                  pl.BlockSpec((tm, tn), lambda i, j, k: (i, j))],
        out_specs=pl.BlockSpec((tm, tn), lambda i, j, k: (i, j)),
        out_shape=jax.ShapeDtypeStruct((M, N), F32),
        scratch_shapes=[pltpu.VMEM((tm, tn), F32)],
        compiler_params=_params(("parallel", "parallel", "arbitrary")),
        name=name,
    )(a, w, r)


def _swiglu_kernel(x_ref, g_ref, wg_ref, wu_ref, o_ref, h_ref):
    @pl.when(pl.program_id(1) == 0)
    def _():
        h_ref[...] = _rms_bf16(x_ref[...], g_ref[...])

    h = h_ref[...]
    gate = jnp.dot(h, wg_ref[...], preferred_element_type=F32)
    up = jnp.dot(h, wu_ref[...], preferred_element_type=F32)
    o_ref[...] = (gate * _sigmoid(gate) * up).astype(o_ref.dtype)


def _swiglu(x2, g, wg, wu, *, tm, tn):
    M, D = x2.shape
    N = wg.shape[1]
    wspec = pl.BlockSpec((D, tn), lambda i, j: (0, j))
    return pl.pallas_call(
        _swiglu_kernel,
        grid=(M // tm, N // tn),
        in_specs=[pl.BlockSpec((tm, D), lambda i, j: (i, 0)),
                  pl.BlockSpec((1, D), lambda i, j: (0, 0)), wspec, wspec],
        out_specs=pl.BlockSpec((tm, tn), lambda i, j: (i, j)),
        out_shape=jax.ShapeDtypeStruct((M, N), BF16),
        scratch_shapes=[pltpu.VMEM((tm, D), BF16)],
        compiler_params=_params(("parallel", "arbitrary")),
        name="swiglu_gate_up",
    )(x2, g, wg, wu)


def _final_norm_kernel(x_ref, g_ref, o_ref):
    x = x_ref[...]
    y = x * lax.rsqrt(jnp.mean(x * x, axis=-1, keepdims=True) + RMS_EPS)
    o_ref[...] = y * g_ref[...]


def _final_norm(x2, g, *, tm):
    M, D = x2.shape
    return pl.pallas_call(
        _final_norm_kernel,
        grid=(M // tm,),
        in_specs=[pl.BlockSpec((tm, D), lambda i: (i, 0)), pl.BlockSpec((1, D), lambda i: (0, 0))],
        out_specs=pl.BlockSpec((tm, D), lambda i: (i, 0)),
        out_shape=jax.ShapeDtypeStruct((M, D), F32),
        compiler_params=_params(("parallel",)),
        name="final_rmsnorm",
    )(x2, g)


def kernel(x, w_in, w_out_a, w_out_b, w_out, w_ff_gate, w_ff_up, w_ff_down,
           g_mix, g_ffn, g_final, rel_bias):
    B, S, D = x.shape
    depth = w_in.shape[0]
    d_ff = w_ff_gate.shape[2]
    M = B * S
    assert S % (BAND * max(DILATIONS)) == 0 and S % (KCHUNK * QB) == 0
    assert w_in.shape[2] == QKV_WIDTH + IDX_WIDTH + 2 * D
    topk = min(TOPK_MAX, S // 4)
    tm = 1024

    band_bias, toep_bias = _bias_tiles(rel_bias, S // QB)
    x2 = x.reshape(M, D)
    for layer in range(depth):
        wl = w_in[layer]
        w_qkv = wl[:, :QKV_WIDTH - WIDTH_B].astype(BF16)
        w_vb = wl[:, QKV_WIDTH - WIDTH_B:QKV_WIDTH].astype(BF16)
        w_idx = jnp.pad(wl[:, QKV_WIDTH:QKV_WIDTH + IDX_WIDTH],
                        ((0, 0), (0, IDX_PAD - IDX_WIDTH))).astype(BF16)
        w_gate = wl[:, QKV_WIDTH + IDX_WIDTH:].astype(BF16)
        g = g_mix[layer].reshape(1, D)

        qkv = _norm_matmul(x2, g, w_qkv, tm=tm, tn=512, out_dtype=BF16, layout="head",
                           batch_seq=(B, S), name="proj_qkv")
        vt = _norm_matmul(x2, g, w_vb, tm=tm, tn=512, out_dtype=BF16, layout="head_t",
                          batch_seq=(B, S), name="proj_vt")
        idx = _norm_matmul(x2, g, w_idx, tm=tm, tn=IDX_PAD, out_dtype=F32, name="proj_idx")
        gates = _norm_matmul(x2, g, w_gate, tm=tm, tn=512, out_dtype=F32, name="proj_gates")

        outs, lses = [], []
        for group in range(len(DILATIONS)):
            o, lse = _dilated_attention(qkv, band_bias, group)
            outs.append(o.reshape(M, WIDTH_A_OUT))
            lses.append(lse.reshape(M, WIDTH_A_OUT))

        mask = _indexer_mask(idx.reshape(B, S, IDX_PAD), topk)
        y_b = _sparse_attention(qkv, vt, toep_bias, mask).reshape(M, WIDTH_B)

        merged = _merge(outs, lses, y_b, gates, w_out_a[layer].astype(BF16),
                        w_out_b[layer].astype(BF16), tm=tm, tn=512)
        x2 = _matmul_residual(merged, w_out[layer].astype(BF16), x2,
                              tm=tm, tn=512, tk=D, name="out_proj")

        act = _swiglu(x2, g_ffn[layer].reshape(1, D), w_ff_gate[layer].astype(BF16),
                      w_ff_up[layer].astype(BF16), tm=tm, tn=512)
        x2 = _matmul_residual(act, w_ff_down[layer].astype(BF16), x2,
                              tm=tm, tn=512, tk=d_ff // 2, name="ffn_down")
    return _final_norm(x2, g_final.reshape(1, D), tm=512).reshape(B, S, D)
```

```python
import functools
import math

import jax
import jax.numpy as jnp
from jax import lax
from jax.experimental import pallas as pl
from jax.experimental.pallas import tpu as pltpu

F32 = jnp.float32
BF16 = jnp.bfloat16

HEAD_DIM = 128
DILATIONS = (1, 4, 16)
BAND = 128
HEADS_PER_GROUP = 4
N_HEADS_A = 12
N_HEADS_B = 8
WIDTH_A = N_HEADS_A * HEAD_DIM
WIDTH_A_OUT = HEADS_PER_GROUP * HEAD_DIM
WIDTH_B = N_HEADS_B * HEAD_DIM
N_IDX_HEADS = 16
IDX_DIM = 64
TOPK_MAX = 256
REL_BUCKETS = 32
REL_MAX_DIST = 2048
RMS_EPS = 1e-6
QKV_WIDTH = 3 * WIDTH_A + 3 * WIDTH_B
IDX_WIDTH = N_IDX_HEADS * IDX_DIM + IDX_DIM + N_IDX_HEADS
IDX_PAD = 1152
KI_COL = N_IDX_HEADS * IDX_DIM
WI_COL = KI_COL + IDX_DIM
LANES = 128
QB = 128
KCHUNK = 4
QT = KCHUNK * QB
TS = BAND * max(DILATIONS)
NEG = -1e30
INT_MIN = -(2 ** 31)
VMEM_LIMIT = 56 * 1024 * 1024


def _params(sem):
    return pltpu.CompilerParams(dimension_semantics=sem, vmem_limit_bytes=VMEM_LIMIT)


def _rms_bf16(x, g):
    y = x * lax.rsqrt(jnp.mean(x * x, axis=-1, keepdims=True) + RMS_EPS)
    return (y * g).astype(BF16)


def _sigmoid(x):
    return 1.0 / (1.0 + jnp.exp(-x))


def _norm_matmul_kernel(x_ref, g_ref, w_ref, o_ref, h_ref, *, layout):
    @pl.when(pl.program_id(1) == 0)
    def _():
        h_ref[...] = _rms_bf16(x_ref[...], g_ref[...])

    acc = jnp.dot(h_ref[...], w_ref[...], preferred_element_type=F32)
    if layout == "rows":
        o_ref[...] = acc.astype(o_ref.dtype)
    else:
        for c in range(acc.shape[1] // HEAD_DIM):
            head = acc[:, c * HEAD_DIM:(c + 1) * HEAD_DIM]
            if layout == "head":
                o_ref[0, c] = head.astype(o_ref.dtype)
            else:
                for r in range(acc.shape[0] // QT):
                    o_ref[0, c, r] = head[r * QT:(r + 1) * QT].T.astype(o_ref.dtype)


def _norm_matmul(x2, g, w, *, tm, tn, out_dtype, layout="rows", batch_seq=None, name):
    M, D = x2.shape
    N = w.shape[1]
    grid = (M // tm, N // tn)
    if layout == "rows":
        out_shape = jax.ShapeDtypeStruct((M, N), out_dtype)
        out_spec = pl.BlockSpec((tm, tn), lambda i, j: (i, j))
    else:
        B, S = batch_seq
        per_b = S // tm
        if layout == "head":
            out_shape = jax.ShapeDtypeStruct((B, N // HEAD_DIM, S, HEAD_DIM), out_dtype)
            out_spec = pl.BlockSpec((1, tn // HEAD_DIM, tm, HEAD_DIM),
                                    lambda i, j: (i // per_b, j, i % per_b, 0))
        else:
            out_shape = jax.ShapeDtypeStruct((B, N // HEAD_DIM, S // QT, HEAD_DIM, QT), out_dtype)
            out_spec = pl.BlockSpec((1, tn // HEAD_DIM, tm // QT, HEAD_DIM, QT),
                                    lambda i, j: (i // per_b, j, i % per_b, 0, 0))
    return pl.pallas_call(
        functools.partial(_norm_matmul_kernel, layout=layout),
        grid=grid,
        in_specs=[pl.BlockSpec((tm, D), lambda i, j: (i, 0)),
                  pl.BlockSpec((1, D), lambda i, j: (0, 0)),
                  pl.BlockSpec((D, tn), lambda i, j: (0, j))],
        out_specs=out_spec,
        out_shape=out_shape,
        scratch_shapes=[pltpu.VMEM((tm, D), BF16)],
        compiler_params=_params(("parallel", "arbitrary")),
        name=name,
    )(x2, g, w)


def _t5_bucket(dist):
    n = jnp.maximum(dist, 0)
    max_exact = REL_BUCKETS // 2
    nf = jnp.maximum(n, 1).astype(F32)
    large = max_exact + (jnp.log(nf / max_exact) / math.log(REL_MAX_DIST / max_exact)
                         * (REL_BUCKETS - max_exact)).astype(jnp.int32)
    large = jnp.minimum(large, REL_BUCKETS - 1)
    return jnp.where(n < max_exact, n, large)


def _table_lookup(tab_ref, bucket, col):
    val = jnp.zeros(bucket.shape, F32)
    for b in range(REL_BUCKETS):
        val = jnp.where(bucket == b, tab_ref[b, col], val)
    return val


def _band_bias_kernel(tab_ref, o_ref):
    h = pl.program_id(0)
    dilation = jnp.left_shift(1, 2 * (h // HEADS_PER_GROUP))
    qi = lax.broadcasted_iota(jnp.int32, (BAND, 2 * BAND), 0)
    kj = lax.broadcasted_iota(jnp.int32, (BAND, 2 * BAND), 1)
    delta = qi + BAND - kj
    bias = _table_lookup(tab_ref, _t5_bucket(delta * dilation), h)
    o_ref[0] = jnp.where((delta >= 0) & (delta <= BAND), bias, -jnp.inf)


def _toeplitz_bias_kernel(tab_ref, o_ref):
    h = pl.program_id(0)
    kj = lax.broadcasted_iota(jnp.int32, (QB, QB), 0)
    qi = lax.broadcasted_iota(jnp.int32, (QB, QB), 1)
    dist = pl.program_id(1) * QB + qi - kj
    o_ref[0, 0] = _table_lookup(tab_ref, _t5_bucket(dist), N_HEADS_A + h)


def _bias_tiles(rel_bias, n_blocks):
    smem = pl.BlockSpec(memory_space=pltpu.SMEM)
    band = pl.pallas_call(
        _band_bias_kernel,
        grid=(N_HEADS_A,),
        in_specs=[smem],
        out_specs=pl.BlockSpec((1, BAND, 2 * BAND), lambda h: (h, 0, 0)),
        out_shape=jax.ShapeDtypeStruct((N_HEADS_A, BAND, 2 * BAND), F32),
        name="band_bias",
    )(rel_bias)
    toep = pl.pallas_call(
        _toeplitz_bias_kernel,
        grid=(N_HEADS_B, n_blocks),
        in_specs=[smem],
        out_specs=pl.BlockSpec((1, 1, QB, QB), lambda h, d: (h, d, 0, 0)),
        out_shape=jax.ShapeDtypeStruct((N_HEADS_B, n_blocks, QB, QB), F32),
        name="toeplitz_bias",
    )(rel_bias)
    return band, toep


def _dilated_kernel(*refs):
    n_g = len(DILATIONS)
    groups = [refs[6 * g:6 * g + 6] for g in range(n_g)]
    y_ref, o_scr, l_scr = refs[6 * n_g:]
    i = pl.program_id(2)
    scale = HEAD_DIM ** -0.5
    col = lax.broadcasted_iota(jnp.int32, (BAND, 2 * BAND), 1)

    for g, d in enumerate(DILATIONS):
        q_ref, k_ref, v_ref, kp_ref, vp_ref, bias_ref = groups[g]
        span = d * BAND
        bias = bias_ref[0]
        bias_first = jnp.where((i > 0) | (col >= BAND), bias, -jnp.inf)
        for r in range(d):
            for n in range(TS // span):
                rows = pl.ds(n * span + r, BAND, stride=d)
                if n == 0:
                    prev = pl.ds(r, BAND, stride=d)
                    k_prev, v_prev, b = kp_ref[0, 0, prev, :], vp_ref[0, 0, prev, :], bias_first
                else:
                    prev = pl.ds((n - 1) * span + r, BAND, stride=d)
                    k_prev, v_prev, b = k_ref[0, 0, prev, :], v_ref[0, 0, prev, :], bias
                kwin = jnp.concatenate([k_prev, k_ref[0, 0, rows, :]], axis=0).astype(BF16)
                vwin = jnp.concatenate([v_prev, v_ref[0, 0, rows, :]], axis=0).astype(BF16)
                q = q_ref[0, 0, rows, :].astype(BF16)
                s = lax.dot_general(q, kwin, (((1,), (1,)), ((), ())), preferred_element_type=F32)
                s = s * scale + b
                m = jnp.max(s, axis=-1, keepdims=True)
                p = jnp.exp(s - m)
                l = jnp.sum(p, axis=-1, keepdims=True)
                o_scr[g, rows, :] = jnp.dot(p.astype(BF16), vwin, preferred_element_type=F32) / l
                l_scr[g, rows, :] = jnp.broadcast_to(m + jnp.log(l), (BAND, HEAD_DIM))

    lses = [l_scr[g] for g in range(n_g)]
    mx = functools.reduce(jnp.maximum, lses)
    es = [jnp.exp(lse - mx) for lse in lses]
    tot = functools.reduce(jnp.add, es)
    y = functools.reduce(jnp.add, [(e / tot) * o_scr[g] for g, e in enumerate(es)])
    y_ref[...] = y.astype(y_ref.dtype)


def _dilated_attention(qkv_a, band_bias):
    B, _, S, _ = qkv_a.shape
    tiles = S // TS
    in_specs, args = [], []
    for g, d in enumerate(DILATIONS):
        span = d * BAND
        per_tile = TS // span
        for base in (0, N_HEADS_A, 2 * N_HEADS_A):
            in_specs.append(pl.BlockSpec(
                (1, 1, TS, HEAD_DIM),
                lambda b, h, i, hd=base + g * HEADS_PER_GROUP: (b, hd + h, i, 0)))
        for base in (N_HEADS_A, 2 * N_HEADS_A):
            in_specs.append(pl.BlockSpec(
                (1, 1, span, HEAD_DIM),
                lambda b, h, i, hd=base + g * HEADS_PER_GROUP, pt=per_tile:
                (b, hd + h, jnp.maximum(i * pt - 1, 0), 0)))
        in_specs.append(pl.BlockSpec((1, BAND, 2 * BAND),
                                     lambda b, h, i, g=g: (g * HEADS_PER_GROUP + h, 0, 0)))
        args += [qkv_a] * 5 + [band_bias]
    return pl.pallas_call(
        _dilated_kernel,
        grid=(B, HEADS_PER_GROUP, tiles),
        in_specs=in_specs,
        out_specs=pl.BlockSpec((TS, HEAD_DIM), lambda b, h, i: (b * tiles + i, h)),
        out_shape=jax.ShapeDtypeStruct((B * S, WIDTH_A_OUT), BF16),
        scratch_shapes=[pltpu.VMEM((len(DILATIONS), TS, HEAD_DIM), F32),
                        pltpu.VMEM((len(DILATIONS), TS, HEAD_DIM), F32)],
        compiler_params=_params(("parallel", "parallel", "parallel")),
        name="dilated_attn",
    )(*args)


def _indexer_kernel(q_ref, k_ref, mask_ref, qh_ref, wb_ref, key_ref, cut_ref, *, n_kb, topk):
    i = pl.program_id(1)
    n_vis = i + 1
    n_chunks = (i + KCHUNK) // KCHUNK
    sub = QB // 8
    for h in range(N_IDX_HEADS):
        qh_ref[h * QB:(h + 1) * QB, :] = q_ref[0, :, h * IDX_DIM:(h + 1) * IDX_DIM].astype(BF16)
    wt = q_ref[0, :, KI_COL:KI_COL + LANES].T * ((N_IDX_HEADS ** -0.5) * (IDX_DIM ** -0.5))
    for h in range(N_IDX_HEADS):
        wb_ref[h] = jnp.broadcast_to(wt[IDX_DIM + h:IDX_DIM + h + 1, :], (8, QB))

    key_pos = (lax.broadcasted_iota(jnp.int32, (sub, 8, QB), 0) * 8
               + lax.broadcasted_iota(jnp.int32, (sub, 8, QB), 1))
    diag_vis = key_pos <= lax.broadcasted_iota(jnp.int32, (sub, 8, QB), 2)

    def score_body(c, carry):
        row0 = pl.multiple_of(c * QT, QT)
        kchunk = k_ref[0, pl.ds(row0, QT), 0:IDX_DIM].astype(BF16)
        accs = [jnp.zeros((sub, 8, QB), F32) for _ in range(KCHUNK)]
        for pair in range(N_IDX_HEADS // 2):
            d = lax.dot_general(kchunk, qh_ref[pair * 2 * QB:(pair + 1) * 2 * QB, :],
                                (((1,), (1,)), ((), ())), preferred_element_type=F32)
            for u in range(2):
                w = wb_ref[2 * pair + u][None]
                for t in range(KCHUNK):
                    tile = d[t * QB:(t + 1) * QB, u * QB:(u + 1) * QB].reshape(sub, 8, QB)
                    accs[t] = accs[t] + jnp.maximum(tile, 0.0) * w
        for t in range(KCHUNK):
            kb = c * KCHUNK + t
            vis = (kb < i) | ((kb == i) & diag_vis)
            bits = pltpu.bitcast(jnp.where(vis, accs[t], -jnp.inf), jnp.int32)
            key_ref[pl.ds(pl.multiple_of(kb * sub, sub), sub)] = bits ^ ((bits >> 31) & 0x7FFFFFFF)
        return carry

    lax.fori_loop(0, n_chunks, score_body, 0)

    def count(pred):
        def count_body(c, cnt):
            for u in range(KCHUNK):
                kb = c * KCHUNK + u
                keys = key_ref[pl.ds(pl.multiple_of(kb * sub, sub), sub)]
                cnt = cnt + jnp.sum(pred(keys, kb).astype(jnp.int32), axis=0)
            return cnt

        cnt = lax.fori_loop(0, n_chunks, count_body, jnp.zeros((8, QB), jnp.int32))
        return jnp.sum(cnt, axis=0, keepdims=True)

    def rows(v):
        return jnp.broadcast_to(v, (8, QB))[None]

    def bit_body(it, state):
        t, cnt_t = state
        cand = t + jnp.left_shift(1, 31 - it)
        cand_b = rows(cand)
        total = count(lambda keys, kb: keys >= cand_b)
        take = total >= topk
        return jnp.where(take, cand, t), jnp.where(take, total, cnt_t)

    n_keys = n_kb * QB
    thr, cnt_thr = lax.fori_loop(
        0, 32, bit_body,
        (jnp.full((1, QB), INT_MIN, jnp.int32), jnp.full((1, QB), n_keys + 1, jnp.int32)))
    thr_b = rows(thr)

    cut_ref[...] = jnp.full((1, QB), n_keys, jnp.int32)

    @pl.when(jnp.max(cnt_thr) > topk)
    def _():
        need = topk - count(lambda keys, kb: keys > thr_b)

        def pos_body(j, u):
            cand = u + jnp.left_shift(1, n_keys.bit_length() - 2 - j)
            cand_b = rows(cand)
            below = count(lambda keys, kb: (keys == thr_b) & (kb * QB + key_pos < cand_b))
            return jnp.where(below < need, cand, u)

        cut_ref[...] = lax.fori_loop(0, n_keys.bit_length() - 1, pos_body,
                                     jnp.zeros((1, QB), jnp.int32))

    cut_b = rows(cut_ref[...])

    def mask_body(kb, carry):
        keys = key_ref[pl.ds(pl.multiple_of(kb * sub, sub), sub)]
        sel = (keys > thr_b) | ((keys == thr_b) & (kb * QB + key_pos <= cut_b))
        sel = sel & ((kb < i) | diag_vis)
        mask_ref[0, 0, kb] = sel.astype(jnp.int32).reshape(QB, QB).astype(jnp.int8)
        return carry

    lax.fori_loop(0, n_vis, mask_body, 0)

    def zero_body(kb, carry):
        mask_ref[0, 0, kb] = jnp.zeros((QB, QB), jnp.int8)
        return carry

    lax.fori_loop(n_vis, n_kb, zero_body, 0)


def _indexer_mask(idx, topk):
    B, S, _ = idx.shape
    n_kb = S // QB
    return pl.pallas_call(
        functools.partial(_indexer_kernel, n_kb=n_kb, topk=topk),
        grid=(B, n_kb),
        in_specs=[pl.BlockSpec((1, QB, IDX_PAD), lambda b, i: (b, i, 0)),
                  pl.BlockSpec((1, S, LANES), lambda b, i: (b, 0, KI_COL // LANES))],
        out_specs=pl.BlockSpec((1, 1, n_kb, QB, QB), lambda b, i: (b, i, 0, 0, 0)),
        out_shape=jax.ShapeDtypeStruct((B, n_kb, n_kb, QB, QB), jnp.int8),
        scratch_shapes=[pltpu.VMEM((N_IDX_HEADS * QB, IDX_DIM), BF16),
                        pltpu.VMEM((N_IDX_HEADS, 8, QB), F32),
                        pltpu.VMEM((n_kb * QB // 8, 8, QB), jnp.int32),
                        pltpu.VMEM((1, QB), jnp.int32)],
        compiler_params=_params(("parallel", "parallel")),
        name="indexer_topk_mask",
    )(idx, idx)


def _sparse_attn_kernel(q_ref, k_ref, vt_ref, bias_ref, mask_ref, o_ref, m_ref, l_ref, acc_ref):
    i = pl.program_id(2)
    scale = HEAD_DIM ** -0.5
    q = q_ref[0, 0]
    m_ref[...] = jnp.full(m_ref.shape, NEG, F32)
    l_ref[...] = jnp.zeros(l_ref.shape, F32)
    acc_ref[...] = jnp.zeros(acc_ref.shape, F32)

    def body(c, carry):
        row0 = pl.multiple_of(c * QT, QT)
        s = lax.dot_general(k_ref[0, 0, pl.ds(row0, QT), :], q, (((1,), (1,)), ((), ())),
                            preferred_element_type=F32) * scale
        rows = []
        for t in range(KCHUNK):
            parts = []
            for a in range(KCHUNK):
                bias = bias_ref[0, jnp.maximum((i - c) * KCHUNK + (a - t), 0)]
                keep = mask_ref[0, a, c * KCHUNK + t].astype(jnp.int32) != 0
                tile = s[t * QB:(t + 1) * QB, a * QB:(a + 1) * QB]
                parts.append(jnp.where(keep, tile + bias, NEG))
            rows.append(jnp.concatenate(parts, axis=1))
        s = jnp.concatenate(rows, axis=0)
        m_old = m_ref[...]
        m_new = jnp.maximum(m_old, jnp.max(s, axis=0, keepdims=True))
        alpha = jnp.exp(m_old - m_new)
        p = jnp.exp(s - m_new)
        m_ref[...] = m_new
        l_ref[...] = alpha * l_ref[...] + jnp.sum(p, axis=0, keepdims=True)
        acc_ref[...] = alpha * acc_ref[...] + jnp.dot(
            vt_ref[0, 0, c], p.astype(BF16), preferred_element_type=F32)
        return carry

    lax.fori_loop(0, i + 1, body, 0)
    o_ref[0] = (acc_ref[...] / l_ref[...]).T.astype(o_ref.dtype)


def _sparse_attention(qkv, vt, toep_bias, mask):
    B, _, S, _ = qkv.shape
    n_kb = S // QB
    q0 = 0

    return pl.pallas_call(
        _sparse_attn_kernel,
        grid=(B, N_HEADS_B, S // QT),
        in_specs=[pl.BlockSpec((1, 1, QT, HEAD_DIM), lambda b, h, i: (b, q0 + h, i, 0)),
                  pl.BlockSpec((1, 1, S, HEAD_DIM), lambda b, h, i: (b, q0 + N_HEADS_B + h, 0, 0)),
                  pl.BlockSpec((1, 1, S // QT, HEAD_DIM, QT), lambda b, h, i: (b, h, 0, 0, 0)),
                  pl.BlockSpec((1, n_kb, QB, QB), lambda b, h, i: (h, 0, 0, 0)),
                  pl.BlockSpec((1, KCHUNK, n_kb, QB, QB), lambda b, h, i: (b, i, 0, 0, 0))],
        out_specs=pl.BlockSpec((1, QT, HEAD_DIM), lambda b, h, i: (b, i, h)),
        out_shape=jax.ShapeDtypeStruct((B, S, WIDTH_B), BF16),
        scratch_shapes=[pltpu.VMEM((1, QT), F32), pltpu.VMEM((1, QT), F32),
                        pltpu.VMEM((HEAD_DIM, QT), F32)],
        compiler_params=_params(("parallel", "parallel", "parallel")),
        name="sparse_attn",
    )(qkv, qkv, vt, toep_bias, mask)


def _merge_kernel(ya_ref, yb_ref, ga_ref, gb_ref, wa_ref, wb_ref, out_ref):
    a = jnp.dot(ya_ref[...], wa_ref[...], preferred_element_type=F32)
    b = jnp.dot(yb_ref[...], wb_ref[...], preferred_element_type=F32)
    out_ref[...] = (_sigmoid(ga_ref[...]) * a + _sigmoid(gb_ref[...]) * b).astype(out_ref.dtype)


def _merge(ya, yb, gates, wa, wb, *, tm, tn):
    M = yb.shape[0]
    N = wa.shape[1]
    return pl.pallas_call(
        _merge_kernel,
        grid=(M // tm, N // tn),
        in_specs=[pl.BlockSpec((tm, WIDTH_A_OUT), lambda i, j: (i, 0)),
                  pl.BlockSpec((tm, WIDTH_B), lambda i, j: (i, 0)),
                  pl.BlockSpec((tm, tn), lambda i, j: (i, j)),
                  pl.BlockSpec((tm, tn), lambda i, j: (i, j + N // tn)),
                  pl.BlockSpec((WIDTH_A_OUT, tn), lambda i, j: (0, j)),
                  pl.BlockSpec((WIDTH_B, tn), lambda i, j: (0, j))],
        out_specs=pl.BlockSpec((tm, tn), lambda i, j: (i, j)),
        out_shape=jax.ShapeDtypeStruct((M, N), BF16),
        compiler_params=_params(("parallel", "parallel")),
        name="gated_merge",
    )(ya, yb, gates, gates, wa, wb)


def _matmul_residual_kernel(a_ref, w_ref, r_ref, o_ref, acc_ref):
    k = pl.program_id(2)

    @pl.when(k == 0)
    def _():
        acc_ref[...] = jnp.zeros_like(acc_ref)

    acc_ref[...] += jnp.dot(a_ref[...], w_ref[...], preferred_element_type=F32)

    @pl.when(k == pl.num_programs(2) - 1)
    def _():
        o_ref[...] = r_ref[...] + acc_ref[...]


def _matmul_residual(a, w, r, *, tm, tn, tk, name):
    M, K = a.shape
    N = w.shape[1]
    return pl.pallas_call(
        _matmul_residual_kernel,
        grid=(M // tm, N // tn, K // tk),
        in_specs=[pl.BlockSpec((tm, tk), lambda i, j, k: (i, k)),
                  pl.BlockSpec((tk, tn), lambda i, j, k: (k, j)),
                  pl.BlockSpec((tm, tn), lambda i, j, k: (i, j))],
        out_specs=pl.BlockSpec((tm, tn), lambda i, j, k: (i, j)),
        out_shape=jax.ShapeDtypeStruct((M, N), F32),
        scratch_shapes=[pltpu.VMEM((tm, tn), F32)],
        compiler_params=_params(("parallel", "parallel", "arbitrary")),
        name=name,
    )(a, w, r)


def _swiglu_kernel(x_ref, g_ref, wg_ref, wu_ref, o_ref, h_ref):
    @pl.when(pl.program_id(1) == 0)
    def _():
        h_ref[...] = _rms_bf16(x_ref[...], g_ref[...])

    h = h_ref[...]
    gate = jnp.dot(h, wg_ref[...], preferred_element_type=F32)
    up = jnp.dot(h, wu_ref[...], preferred_element_type=F32)
    o_ref[...] = (gate * _sigmoid(gate) * up).astype(o_ref.dtype)


def _swiglu(x2, g, wg, wu, *, tm, tn):
    M, D = x2.shape
    N = wg.shape[1]
    wspec = pl.BlockSpec((D, tn), lambda i, j: (0, j))
    return pl.pallas_call(
        _swiglu_kernel,
        grid=(M // tm, N // tn),
        in_specs=[pl.BlockSpec((tm, D), lambda i, j: (i, 0)),
                  pl.BlockSpec((1, D), lambda i, j: (0, 0)), wspec, wspec],
        out_specs=pl.BlockSpec((tm, tn), lambda i, j: (i, j)),
        out_shape=jax.ShapeDtypeStruct((M, N), BF16),
        scratch_shapes=[pltpu.VMEM((tm, D), BF16)],
        compiler_params=_params(("parallel", "arbitrary")),
        name="swiglu_gate_up",
    )(x2, g, wg, wu)


def _final_norm_kernel(x_ref, g_ref, o_ref):
    x = x_ref[...]
    y = x * lax.rsqrt(jnp.mean(x * x, axis=-1, keepdims=True) + RMS_EPS)
    o_ref[...] = y * g_ref[...]


def _final_norm(x2, g, *, tm):
    M, D = x2.shape
    return pl.pallas_call(
        _final_norm_kernel,
        grid=(M // tm,),
        in_specs=[pl.BlockSpec((tm, D), lambda i: (i, 0)), pl.BlockSpec((1, D), lambda i: (0, 0))],
        out_specs=pl.BlockSpec((tm, D), lambda i: (i, 0)),
        out_shape=jax.ShapeDtypeStruct((M, D), F32),
        compiler_params=_params(("parallel",)),
        name="final_rmsnorm",
    )(x2, g)


def kernel(x, w_in, w_out_a, w_out_b, w_out, w_ff_gate, w_ff_up, w_ff_down,
           g_mix, g_ffn, g_final, rel_bias):
    B, S, D = x.shape
    depth = w_in.shape[0]
    d_ff = w_ff_gate.shape[2]
    M = B * S
    assert S % (BAND * max(DILATIONS)) == 0 and S % (KCHUNK * QB) == 0
    assert w_in.shape[2] == QKV_WIDTH + IDX_WIDTH + 2 * D
    topk = min(TOPK_MAX, S // 4)
    tm = 1024

    band_bias, toep_bias = _bias_tiles(rel_bias, S // QB)
    x2 = x.reshape(M, D)
    for layer in range(depth):
        wl = w_in[layer]
        w_a = wl[:, :3 * WIDTH_A].astype(BF16)
        w_qkb = wl[:, 3 * WIDTH_A:QKV_WIDTH - WIDTH_B].astype(BF16)
        w_vb = wl[:, QKV_WIDTH - WIDTH_B:QKV_WIDTH].astype(BF16)
        w_idx = jnp.pad(wl[:, QKV_WIDTH:QKV_WIDTH + IDX_WIDTH],
                        ((0, 0), (0, IDX_PAD - IDX_WIDTH))).astype(BF16)
        w_gate = wl[:, QKV_WIDTH + IDX_WIDTH:].astype(BF16)
        g = g_mix[layer].reshape(1, D)

        qkv_a = _norm_matmul(x2, g, w_a, tm=tm, tn=512, out_dtype=F32, layout="head",
                             batch_seq=(B, S), name="proj_a")
        qk_b = _norm_matmul(x2, g, w_qkb, tm=tm, tn=512, out_dtype=BF16, layout="head",
                            batch_seq=(B, S), name="proj_qk_b")
        vt = _norm_matmul(x2, g, w_vb, tm=tm, tn=512, out_dtype=BF16, layout="head_t",
                          batch_seq=(B, S), name="proj_vt")
        idx = _norm_matmul(x2, g, w_idx, tm=tm, tn=IDX_PAD, out_dtype=F32, name="proj_idx")
        gates = _norm_matmul(x2, g, w_gate, tm=tm, tn=512, out_dtype=F32, name="proj_gates")

        y_a = _dilated_attention(qkv_a, band_bias)
        mask = _indexer_mask(idx.reshape(B, S, IDX_PAD), topk)
        y_b = _sparse_attention(qk_b, vt, toep_bias, mask).reshape(M, WIDTH_B)

        merged = _merge(y_a, y_b, gates, w_out_a[layer].astype(BF16),
                        w_out_b[layer].astype(BF16), tm=tm, tn=512)
        x2 = _matmul_residual(merged, w_out[layer].astype(BF16), x2,
                              tm=tm, tn=512, tk=D, name="out_proj")

        act = _swiglu(x2, g_ffn[layer].reshape(1, D), w_ff_gate[layer].astype(BF16),
                      w_ff_up[layer].astype(BF16), tm=tm, tn=512)
        x2 = _matmul_residual(act, w_ff_down[layer].astype(BF16), x2,
                              tm=tm, tn=512, tk=d_ff // 2, name="ffn_down")
    return _final_norm(x2, g_final.reshape(1, D), tm=512).reshape(B, S, D)
```

```python
import functools
import math

import jax
import jax.numpy as jnp
from jax import lax
from jax.experimental import pallas as pl
from jax.experimental.pallas import tpu as pltpu

F32 = jnp.float32
BF16 = jnp.bfloat16

HEAD_DIM = 128
DILATIONS = (1, 4, 16)
BAND = 128
HEADS_PER_GROUP = 4
N_HEADS_A = 12
N_HEADS_B = 8
WIDTH_A = N_HEADS_A * HEAD_DIM
WIDTH_A_OUT = HEADS_PER_GROUP * HEAD_DIM
WIDTH_B = N_HEADS_B * HEAD_DIM
N_IDX_HEADS = 16
IDX_DIM = 64
TOPK_MAX = 256
REL_BUCKETS = 32
REL_MAX_DIST = 2048
RMS_EPS = 1e-6
QKV_WIDTH = 3 * WIDTH_A + 3 * WIDTH_B
IDX_WIDTH = N_IDX_HEADS * IDX_DIM + IDX_DIM + N_IDX_HEADS
IDX_PAD = 1152
KI_COL = N_IDX_HEADS * IDX_DIM
WI_COL = KI_COL + IDX_DIM
LANES = 128
QB = 128
KCHUNK = 4
QT = KCHUNK * QB
TS = BAND * max(DILATIONS)
NEG = -(2.0 ** 100)
INT_MIN = -(2 ** 31)
LOG2E = math.log2(math.e)
ONES_ROWS = 16
VMEM_LIMIT = 56 * 1024 * 1024


def _params(sem):
    return pltpu.CompilerParams(dimension_semantics=sem, vmem_limit_bytes=VMEM_LIMIT)


def _sigmoid(x):
    return 1.0 / (1.0 + jnp.exp(-x))


def _rmsnorm_kernel(x_ref, g_ref, o_ref):
    x = x_ref[...]
    y = x * lax.rsqrt(jnp.mean(x * x, axis=-1, keepdims=True) + RMS_EPS)
    o_ref[...] = (y * g_ref[...]).astype(o_ref.dtype)


def _rmsnorm(x2, g, *, tm, out_dtype, name):
    M, D = x2.shape
    return pl.pallas_call(
        _rmsnorm_kernel,
        grid=(M // tm,),
        in_specs=[pl.BlockSpec((tm, D), lambda i: (i, 0)), pl.BlockSpec((1, D), lambda i: (0, 0))],
        out_specs=pl.BlockSpec((tm, D), lambda i: (i, 0)),
        out_shape=jax.ShapeDtypeStruct((M, D), out_dtype),
        compiler_params=_params(("parallel",)),
        name=name,
    )(x2, g.reshape(1, D))


def _matmul_kernel(a_ref, w_ref, o_ref, *, layout):
    acc = jnp.dot(a_ref[...], w_ref[...], preferred_element_type=F32)
    if layout == "rows":
        o_ref[...] = acc.astype(o_ref.dtype)
    else:
        for c in range(acc.shape[1] // HEAD_DIM):
            head = acc[:, c * HEAD_DIM:(c + 1) * HEAD_DIM]
            if layout == "head":
                o_ref[0, c] = head.astype(o_ref.dtype)
            else:
                for r in range(acc.shape[0] // QT):
                    o_ref[0, c, r] = head[r * QT:(r + 1) * QT].T.astype(o_ref.dtype)


def _matmul(a, w, *, tm, tn, out_dtype, layout="rows", batch_seq=None, name):
    M, K = a.shape
    N = w.shape[1]
    if layout == "rows":
        out_shape = jax.ShapeDtypeStruct((M, N), out_dtype)
        out_spec = pl.BlockSpec((tm, tn), lambda i, j: (i, j))
    else:
        B, S = batch_seq
        per_b = S // tm
        if layout == "head":
            out_shape = jax.ShapeDtypeStruct((B, N // HEAD_DIM, S, HEAD_DIM), out_dtype)
            out_spec = pl.BlockSpec((1, tn // HEAD_DIM, tm, HEAD_DIM),
                                    lambda i, j: (i // per_b, j, i % per_b, 0))
        else:
            out_shape = jax.ShapeDtypeStruct((B, N // HEAD_DIM, S // QT, HEAD_DIM, QT), out_dtype)
            out_spec = pl.BlockSpec((1, tn // HEAD_DIM, tm // QT, HEAD_DIM, QT),
                                    lambda i, j: (i // per_b, j, i % per_b, 0, 0))
    return pl.pallas_call(
        functools.partial(_matmul_kernel, layout=layout),
        grid=(M // tm, N // tn),
        in_specs=[pl.BlockSpec((tm, K), lambda i, j: (i, 0)),
                  pl.BlockSpec((K, tn), lambda i, j: (0, j))],
        out_specs=out_spec,
        out_shape=out_shape,
        compiler_params=_params(("parallel", "parallel")),
        name=name,
    )(a, w)


def _t5_bucket(dist):
    n = jnp.maximum(dist, 0)
    max_exact = REL_BUCKETS // 2
    nf = jnp.maximum(n, 1).astype(F32)
    large = max_exact + (jnp.log(nf / max_exact) / math.log(REL_MAX_DIST / max_exact)
                         * (REL_BUCKETS - max_exact)).astype(jnp.int32)
    large = jnp.minimum(large, REL_BUCKETS - 1)
    return jnp.where(n < max_exact, n, large)


def _table_lookup(tab_ref, bucket, col):
    val = jnp.zeros(bucket.shape, F32)
    for b in range(REL_BUCKETS):
        val = jnp.where(bucket == b, tab_ref[b, col], val)
    return val


def _band_bias_kernel(tab_ref, o_ref):
    h = pl.program_id(0)
    dilation = jnp.left_shift(1, 2 * (h // HEADS_PER_GROUP))
    qi = lax.broadcasted_iota(jnp.int32, (BAND, 2 * BAND), 0)
    kj = lax.broadcasted_iota(jnp.int32, (BAND, 2 * BAND), 1)
    delta = qi + BAND - kj
    bias = _table_lookup(tab_ref, _t5_bucket(delta * dilation), h)
    o_ref[0] = jnp.where((delta >= 0) & (delta <= BAND), bias, -jnp.inf)


def _toeplitz_bias_kernel(tab_ref, o_ref):
    h = pl.program_id(0)
    kj = lax.broadcasted_iota(jnp.int32, (QB, QB), 0)
    qi = lax.broadcasted_iota(jnp.int32, (QB, QB), 1)
    dist = pl.program_id(1) * QB + qi - kj
    o_ref[0, 0] = _table_lookup(tab_ref, _t5_bucket(dist), N_HEADS_A + h) * LOG2E


def _bias_tiles(rel_bias, n_blocks):
    smem = pl.BlockSpec(memory_space=pltpu.SMEM)
    band = pl.pallas_call(
        _band_bias_kernel,
        grid=(N_HEADS_A,),
        in_specs=[smem],
        out_specs=pl.BlockSpec((1, BAND, 2 * BAND), lambda h: (h, 0, 0)),
        out_shape=jax.ShapeDtypeStruct((N_HEADS_A, BAND, 2 * BAND), F32),
        name="band_bias",
    )(rel_bias)
    toep = pl.pallas_call(
        _toeplitz_bias_kernel,
        grid=(N_HEADS_B, n_blocks),
        in_specs=[smem],
        out_specs=pl.BlockSpec((1, 1, QB, QB), lambda h, d: (h, d, 0, 0)),
        out_shape=jax.ShapeDtypeStruct((N_HEADS_B, n_blocks, QB, QB), F32),
        name="toeplitz_bias",
    )(rel_bias)
    return band, toep


def _dilated_kernel(*refs):
    n_g = len(DILATIONS)
    groups = [refs[6 * g:6 * g + 6] for g in range(n_g)]
    y_ref, o_scr, l_scr = refs[6 * n_g:]
    i = pl.program_id(2)
    scale = HEAD_DIM ** -0.5
    col = lax.broadcasted_iota(jnp.int32, (BAND, 2 * BAND), 1)

    for g, d in enumerate(DILATIONS):
        q_ref, k_ref, v_ref, kp_ref, vp_ref, bias_ref = groups[g]
        span = d * BAND
        bias = bias_ref[0]
        bias_first = jnp.where((i > 0) | (col >= BAND), bias, -jnp.inf)
        for r in range(d):
            for n in range(TS // span):
                rows = pl.ds(n * span + r, BAND, stride=d)
                if n == 0:
                    prev = pl.ds(r, BAND, stride=d)
                    k_prev, v_prev, b = kp_ref[0, 0, prev, :], vp_ref[0, 0, prev, :], bias_first
                else:
                    prev = pl.ds((n - 1) * span + r, BAND, stride=d)
                    k_prev, v_prev, b = k_ref[0, 0, prev, :], v_ref[0, 0, prev, :], bias
                kwin = jnp.concatenate([k_prev, k_ref[0, 0, rows, :]], axis=0).astype(BF16)
                vwin = jnp.concatenate([v_prev, v_ref[0, 0, rows, :]], axis=0).astype(BF16)
                q = q_ref[0, 0, rows, :].astype(BF16)
                s = lax.dot_general(q, kwin, (((1,), (1,)), ((), ())), preferred_element_type=F32)
                s = s * scale + b
                m = jnp.max(s, axis=-1, keepdims=True)
                p = jnp.exp(s - m)
                l = jnp.sum(p, axis=-1, keepdims=True)
                o_scr[g, rows, :] = jnp.dot(p.astype(BF16), vwin, preferred_element_type=F32) / l
                l_scr[g, rows, :] = jnp.broadcast_to(m + jnp.log(l), (BAND, HEAD_DIM))

    lses = [l_scr[g] for g in range(n_g)]
    mx = functools.reduce(jnp.maximum, lses)
    es = [jnp.exp(lse - mx) for lse in lses]
    tot = functools.reduce(jnp.add, es)
    y = functools.reduce(jnp.add, [(e / tot) * o_scr[g] for g, e in enumerate(es)])
    y_ref[...] = y.astype(y_ref.dtype)


def _dilated_attention(qkv_a, band_bias):
    B, _, S, _ = qkv_a.shape
    tiles = S // TS
    in_specs, args = [], []
    for g, d in enumerate(DILATIONS):
        span = d * BAND
        per_tile = TS // span
        for base in (0, N_HEADS_A, 2 * N_HEADS_A):
            in_specs.append(pl.BlockSpec(
                (1, 1, TS, HEAD_DIM),
                lambda b, h, i, hd=base + g * HEADS_PER_GROUP: (b, hd + h, i, 0)))
        for base in (N_HEADS_A, 2 * N_HEADS_A):
            in_specs.append(pl.BlockSpec(
                (1, 1, span, HEAD_DIM),
                lambda b, h, i, hd=base + g * HEADS_PER_GROUP, pt=per_tile:
                (b, hd + h, jnp.maximum(i * pt - 1, 0), 0)))
        in_specs.append(pl.BlockSpec((1, BAND, 2 * BAND),
                                     lambda b, h, i, g=g: (g * HEADS_PER_GROUP + h, 0, 0)))
        args += [qkv_a] * 5 + [band_bias]
    return pl.pallas_call(
        _dilated_kernel,
        grid=(B, HEADS_PER_GROUP, tiles),
        in_specs=in_specs,
        out_specs=pl.BlockSpec((TS, HEAD_DIM), lambda b, h, i: (b * tiles + i, h)),
        out_shape=jax.ShapeDtypeStruct((B * S, WIDTH_A_OUT), BF16),
        scratch_shapes=[pltpu.VMEM((len(DILATIONS), TS, HEAD_DIM), F32),
                        pltpu.VMEM((len(DILATIONS), TS, HEAD_DIM), F32)],
        compiler_params=_params(("parallel", "parallel", "parallel")),
        name="dilated_attn",
    )(*args)


def _indexer_kernel(q_ref, k_ref, mask_ref, qh_ref, wb_ref, key_ref, cut_ref, *, n_kb, topk):
    i = pl.program_id(1)
    n_vis = i + 1
    n_chunks = (i + KCHUNK) // KCHUNK
    sub = QB // 8
    for h in range(N_IDX_HEADS):
        qh_ref[h * QB:(h + 1) * QB, :] = q_ref[0, :, h * IDX_DIM:(h + 1) * IDX_DIM].astype(BF16)
    wt = q_ref[0, :, KI_COL:KI_COL + LANES].T * ((N_IDX_HEADS ** -0.5) * (IDX_DIM ** -0.5))
    for h in range(N_IDX_HEADS):
        wb_ref[h] = jnp.broadcast_to(wt[IDX_DIM + h:IDX_DIM + h + 1, :], (8, QB))

    key_pos = (lax.broadcasted_iota(jnp.int32, (sub, 8, QB), 0) * 8
               + lax.broadcasted_iota(jnp.int32, (sub, 8, QB), 1))
    diag_vis = key_pos <= lax.broadcasted_iota(jnp.int32, (sub, 8, QB), 2)

    def score_body(c, carry):
        row0 = pl.multiple_of(c * QT, QT)
        kchunk = k_ref[0, pl.ds(row0, QT), 0:IDX_DIM].astype(BF16)
        accs = [jnp.zeros((sub, 8, QB), F32) for _ in range(KCHUNK)]
        for pair in range(N_IDX_HEADS // 2):
            d = lax.dot_general(kchunk, qh_ref[pair * 2 * QB:(pair + 1) * 2 * QB, :],
                                (((1,), (1,)), ((), ())), preferred_element_type=F32)
            for u in range(2):
                w = wb_ref[2 * pair + u][None]
                for t in range(KCHUNK):
                    tile = d[t * QB:(t + 1) * QB, u * QB:(u + 1) * QB].reshape(sub, 8, QB)
                    accs[t] = accs[t] + jnp.maximum(tile, 0.0) * w
        for t in range(KCHUNK):
            kb = c * KCHUNK + t
            vis = (kb < i) | ((kb == i) & diag_vis)
            bits = pltpu.bitcast(jnp.where(vis, accs[t], -jnp.inf), jnp.int32)
            key_ref[pl.ds(pl.multiple_of(kb * sub, sub), sub)] = bits ^ ((bits >> 31) & 0x7FFFFFFF)
        return carry

    lax.fori_loop(0, n_chunks, score_body, 0)

    def count(pred):
        def count_body(c, cnt):
            for u in range(KCHUNK):
                kb = c * KCHUNK + u
                keys = key_ref[pl.ds(pl.multiple_of(kb * sub, sub), sub)]
                cnt = cnt + jnp.sum(pred(keys, kb).astype(jnp.int32), axis=0)
            return cnt

        cnt = lax.fori_loop(0, n_chunks, count_body, jnp.zeros((8, QB), jnp.int32))
        return jnp.sum(cnt, axis=0, keepdims=True)

    def rows(v):
        return jnp.broadcast_to(v, (8, QB))[None]

    def bit_body(it, state):
        t, cnt_t = state
        cand = t + jnp.left_shift(1, 31 - it)
        cand_b = rows(cand)
        total = count(lambda keys, kb: keys >= cand_b)
        take = total >= topk
        return jnp.where(take, cand, t), jnp.where(take, total, cnt_t)

    n_keys = n_kb * QB
    thr, cnt_thr = lax.fori_loop(
        0, 32, bit_body,
        (jnp.full((1, QB), INT_MIN, jnp.int32), jnp.full((1, QB), n_keys + 1, jnp.int32)))
    thr_b = rows(thr)

    cut_ref[...] = jnp.full((1, QB), n_keys, jnp.int32)

    @pl.when(jnp.max(cnt_thr) > topk)
    def _():
        need = topk - count(lambda keys, kb: keys > thr_b)

        def pos_body(j, u):
            cand = u + jnp.left_shift(1, n_keys.bit_length() - 2 - j)
            cand_b = rows(cand)
            below = count(lambda keys, kb: (keys == thr_b) & (kb * QB + key_pos < cand_b))
            return jnp.where(below < need, cand, u)

        cut_ref[...] = lax.fori_loop(0, n_keys.bit_length() - 1, pos_body,
                                     jnp.zeros((1, QB), jnp.int32))

    cut_b = rows(cut_ref[...])

    def mask_body(kb, carry):
        keys = key_ref[pl.ds(pl.multiple_of(kb * sub, sub), sub)]
        sel = (keys > thr_b) | ((keys == thr_b) & (kb * QB + key_pos <= cut_b))
        sel = sel & ((kb < i) | diag_vis)
        mask_ref[0, 0, kb] = jnp.where(sel, 0.0, NEG).reshape(QB, QB).astype(mask_ref.dtype)
        return carry

    lax.fori_loop(0, n_vis, mask_body, 0)

    def zero_body(kb, carry):
        mask_ref[0, 0, kb] = jnp.full((QB, QB), NEG, mask_ref.dtype)
        return carry

    lax.fori_loop(n_vis, n_kb, zero_body, 0)


def _indexer_mask(idx, topk):
    B, S, _ = idx.shape
    n_kb = S // QB
    return pl.pallas_call(
        functools.partial(_indexer_kernel, n_kb=n_kb, topk=topk),
        grid=(B, n_kb),
        in_specs=[pl.BlockSpec((1, QB, IDX_PAD), lambda b, i: (b, i, 0)),
                  pl.BlockSpec((1, S, LANES), lambda b, i: (b, 0, KI_COL // LANES))],
        out_specs=pl.BlockSpec((1, 1, n_kb, QB, QB), lambda b, i: (b, i, 0, 0, 0)),
        out_shape=jax.ShapeDtypeStruct((B, n_kb, n_kb, QB, QB), BF16),
        scratch_shapes=[pltpu.VMEM((N_IDX_HEADS * QB, IDX_DIM), BF16),
                        pltpu.VMEM((N_IDX_HEADS, 8, QB), F32),
                        pltpu.VMEM((n_kb * QB // 8, 8, QB), jnp.int32),
                        pltpu.VMEM((1, QB), jnp.int32)],
        compiler_params=_params(("parallel", "parallel")),
        name="indexer_topk_mask",
    )(idx, idx)


def _sparse_attn_kernel(q_ref, k_ref, vt_ref, bias_ref, mask_ref, o_ref, m_ref, acc_ref):
    i = pl.program_id(2)
    scale = HEAD_DIM ** -0.5 * LOG2E
    q = q_ref[0, 0]
    m_ref[...] = jnp.full(m_ref.shape, NEG, F32)
    acc_ref[...] = jnp.zeros(acc_ref.shape, F32)
    ones = jnp.ones((ONES_ROWS, QT), BF16)

    def body(c, carry):
        row0 = pl.multiple_of(c * QT, QT)
        s = lax.dot_general(k_ref[0, 0, pl.ds(row0, QT), :], q, (((1,), (1,)), ((), ())),
                            preferred_element_type=F32) * scale
        rows = []
        for t in range(KCHUNK):
            parts = []
            for a in range(KCHUNK):
                bias = bias_ref[0, jnp.maximum((i - c) * KCHUNK + (a - t), 0)]
                drop = mask_ref[0, a, c * KCHUNK + t].astype(F32)
                parts.append(s[t * QB:(t + 1) * QB, a * QB:(a + 1) * QB] + bias + drop)
            rows.append(jnp.concatenate(parts, axis=1))
        s = jnp.concatenate(rows, axis=0)
        m_old = m_ref[...]
        m_new = jnp.maximum(m_old, jnp.max(s, axis=0, keepdims=True))
        alpha = jnp.exp2(m_old - m_new)
        p = jnp.exp2(s - m_new)
        m_ref[...] = m_new
        v_ones = jnp.concatenate([vt_ref[0, 0, c], ones], axis=0)
        acc_ref[...] = alpha * acc_ref[...] + jnp.dot(v_ones, p.astype(BF16),
                                                      preferred_element_type=F32)
        return carry

    lax.fori_loop(0, i + 1, body, 0)
    out = acc_ref[0:HEAD_DIM, :] / acc_ref[HEAD_DIM:HEAD_DIM + 1, :]
    o_ref[0] = out.T.astype(o_ref.dtype)


def _sparse_attention(qk, vt, toep_bias, mask):
    B, _, S, _ = qk.shape
    n_kb = S // QB
    return pl.pallas_call(
        _sparse_attn_kernel,
        grid=(B, N_HEADS_B, S // QT),
        in_specs=[pl.BlockSpec((1, 1, QT, HEAD_DIM), lambda b, h, i: (b, h, i, 0)),
                  pl.BlockSpec((1, 1, S, HEAD_DIM), lambda b, h, i: (b, N_HEADS_B + h, 0, 0)),
                  pl.BlockSpec((1, 1, S // QT, HEAD_DIM, QT), lambda b, h, i: (b, h, 0, 0, 0)),
                  pl.BlockSpec((1, n_kb, QB, QB), lambda b, h, i: (h, 0, 0, 0)),
                  pl.BlockSpec((1, KCHUNK, n_kb, QB, QB), lambda b, h, i: (b, i, 0, 0, 0))],
        out_specs=pl.BlockSpec((1, QT, HEAD_DIM), lambda b, h, i: (b, i, h)),
        out_shape=jax.ShapeDtypeStruct((B, S, WIDTH_B), BF16),
        scratch_shapes=[pltpu.VMEM((1, QT), F32), pltpu.VMEM((HEAD_DIM + ONES_ROWS, QT), F32)],
        compiler_params=_params(("parallel", "parallel", "parallel")),
        name="sparse_attn",
    )(qk, qk, vt, toep_bias, mask)


def _merge_kernel(ya_ref, yb_ref, ga_ref, gb_ref, wa_ref, wb_ref, out_ref):
    a = jnp.dot(ya_ref[...], wa_ref[...], preferred_element_type=F32)
    b = jnp.dot(yb_ref[...], wb_ref[...], preferred_element_type=F32)
    out_ref[...] = (_sigmoid(ga_ref[...]) * a + _sigmoid(gb_ref[...]) * b).astype(out_ref.dtype)


def _merge(ya, yb, gates, wa, wb, *, tm, tn):
    M = yb.shape[0]
    N = wa.shape[1]
    return pl.pallas_call(
        _merge_kernel,
        grid=(M // tm, N // tn),
        in_specs=[pl.BlockSpec((tm, WIDTH_A_OUT), lambda i, j: (i, 0)),
                  pl.BlockSpec((tm, WIDTH_B), lambda i, j: (i, 0)),
                  pl.BlockSpec((tm, tn), lambda i, j: (i, j)),
                  pl.BlockSpec((tm, tn), lambda i, j: (i, j + N // tn)),
                  pl.BlockSpec((WIDTH_A_OUT, tn), lambda i, j: (0, j)),
                  pl.BlockSpec((WIDTH_B, tn), lambda i, j: (0, j))],
        out_specs=pl.BlockSpec((tm, tn), lambda i, j: (i, j)),
        out_shape=jax.ShapeDtypeStruct((M, N), BF16),
        compiler_params=_params(("parallel", "parallel")),
        name="gated_merge",
    )(ya, yb, gates, gates, wa, wb)


def _matmul_residual_kernel(a_ref, w_ref, r_ref, o_ref):
    o_ref[...] = r_ref[...] + jnp.dot(a_ref[...], w_ref[...], preferred_element_type=F32)


def _matmul_residual(a, w, r, *, tm, tn, name):
    M, K = a.shape
    N = w.shape[1]
    return pl.pallas_call(
        _matmul_residual_kernel,
        grid=(M // tm, N // tn),
        in_specs=[pl.BlockSpec((tm, K), lambda i, j: (i, 0)),
                  pl.BlockSpec((K, tn), lambda i, j: (0, j)),
                  pl.BlockSpec((tm, tn), lambda i, j: (i, j))],
        out_specs=pl.BlockSpec((tm, tn), lambda i, j: (i, j)),
        out_shape=jax.ShapeDtypeStruct((M, N), F32),
        compiler_params=_params(("parallel", "parallel")),
        name=name,
    )(a, w, r)


def _swiglu_kernel(h_ref, wg_ref, wu_ref, o_ref):
    h = h_ref[...]
    gate = jnp.dot(h, wg_ref[...], preferred_element_type=F32)
    up = jnp.dot(h, wu_ref[...], preferred_element_type=F32)
    o_ref[...] = (gate * _sigmoid(gate) * up).astype(o_ref.dtype)


def _swiglu(h, wg, wu, *, tm, tn):
    M, D = h.shape
    N = wg.shape[1]
    wspec = pl.BlockSpec((D, tn), lambda i, j: (0, j))
    return pl.pallas_call(
        _swiglu_kernel,
        grid=(M // tm, N // tn),
        in_specs=[pl.BlockSpec((tm, D), lambda i, j: (i, 0)), wspec, wspec],
        out_specs=pl.BlockSpec((tm, tn), lambda i, j: (i, j)),
        out_shape=jax.ShapeDtypeStruct((M, N), BF16),
        compiler_params=_params(("parallel", "parallel")),
        name="swiglu_gate_up",
    )(h, wg, wu)


def kernel(x, w_in, w_out_a, w_out_b, w_out, w_ff_gate, w_ff_up, w_ff_down,
           g_mix, g_ffn, g_final, rel_bias):
    B, S, D = x.shape
    depth = w_in.shape[0]
    M = B * S
    assert S % TS == 0 and S % QT == 0
    assert w_in.shape[2] == QKV_WIDTH + IDX_WIDTH + 2 * D
    topk = min(TOPK_MAX, S // 4)
    tm, tm_small, tn = 2048, 1024, 512

    band_bias, toep_bias = _bias_tiles(rel_bias, S // QB)
    x2 = x.reshape(M, D)
    h = _rmsnorm(x2, g_mix[0], tm=tm_small, out_dtype=BF16, name="rmsnorm_mix")
    for layer in range(depth):
        wl = w_in[layer]
        w_a = wl[:, :3 * WIDTH_A].astype(BF16)
        w_qkb = wl[:, 3 * WIDTH_A:QKV_WIDTH - WIDTH_B].astype(BF16)
        w_vb = wl[:, QKV_WIDTH - WIDTH_B:QKV_WIDTH].astype(BF16)
        w_idx = jnp.pad(wl[:, QKV_WIDTH:QKV_WIDTH + IDX_WIDTH],
                        ((0, 0), (0, IDX_PAD - IDX_WIDTH))).astype(BF16)
        w_gate = wl[:, QKV_WIDTH + IDX_WIDTH:].astype(BF16)

        qkv_a = _matmul(h, w_a, tm=tm, tn=tn, out_dtype=F32, layout="head",
                        batch_seq=(B, S), name="proj_a")
        qk_b = _matmul(h, w_qkb, tm=tm, tn=tn, out_dtype=BF16, layout="head",
                       batch_seq=(B, S), name="proj_qk_b")
        vt = _matmul(h, w_vb, tm=tm_small, tn=tn, out_dtype=BF16, layout="head_t",
                     batch_seq=(B, S), name="proj_vt")
        idx = _matmul(h, w_idx, tm=tm_small, tn=IDX_PAD, out_dtype=F32, name="proj_idx")
        gates = _matmul(h, w_gate, tm=tm, tn=tn, out_dtype=F32, name="proj_gates")

        y_a = _dilated_attention(qkv_a, band_bias)
        mask = _indexer_mask(idx.reshape(B, S, IDX_PAD), topk)
        y_b = _sparse_attention(qk_b, vt, toep_bias, mask).reshape(M, WIDTH_B)

        merged = _merge(y_a, y_b, gates, w_out_a[layer].astype(BF16),
                        w_out_b[layer].astype(BF16), tm=tm, tn=tn)
        x2 = _matmul_residual(merged, w_out[layer].astype(BF16), x2, tm=tm, tn=tn,
                              name="out_proj")

        h = _rmsnorm(x2, g_ffn[layer], tm=tm_small, out_dtype=BF16, name="rmsnorm_ffn")
        act = _swiglu(h, w_ff_gate[layer].astype(BF16), w_ff_up[layer].astype(BF16),
                      tm=tm, tn=tn)
        x2 = _matmul_residual(act, w_ff_down[layer].astype(BF16), x2, tm=tm_small, tn=tn,
                              name="ffn_down")
        if layer + 1 < depth:
            h = _rmsnorm(x2, g_mix[layer + 1], tm=tm_small, out_dtype=BF16, name="rmsnorm_mix")
    return _rmsnorm(x2, g_final, tm=tm_small, out_dtype=F32, name="final_rmsnorm").reshape(B, S, D)
```

```python
import functools
import math
from typing import NamedTuple

import jax
import jax.numpy as jnp
from jax import lax
from jax.experimental import pallas as pl
from jax.experimental.pallas import tpu as pltpu

F32 = jnp.float32
BF16 = jnp.bfloat16

HEAD_DIM = 128
DILATIONS = (1, 4, 16)
BAND = 128
HEADS_PER_GROUP = 4
N_HEADS_A = 12
N_HEADS_B = 8
WIDTH_A = N_HEADS_A * HEAD_DIM
WIDTH_A_OUT = HEADS_PER_GROUP * HEAD_DIM
WIDTH_B = N_HEADS_B * HEAD_DIM
N_IDX_HEADS = 16
IDX_DIM = 64
TOPK_MAX = 256
REL_BUCKETS = 32
REL_MAX_DIST = 2048
RMS_EPS = 1e-6
QKV_WIDTH = 3 * WIDTH_A + 3 * WIDTH_B
IDX_WIDTH = N_IDX_HEADS * IDX_DIM + IDX_DIM + N_IDX_HEADS
IDX_PAD = 1152
KI_COL = N_IDX_HEADS * IDX_DIM
WI_COL = KI_COL + IDX_DIM
LANES = 128
QB = 128
KCHUNK = 4
QT = KCHUNK * QB
QTILE = 2 * QT
TS = BAND * max(DILATIONS)
NEG = -(2.0 ** 100)
INT_MIN = -(2 ** 31)
LOG2E = math.log2(math.e)
ONES_ROWS = 16
VMEM_LIMIT = 56 * 1024 * 1024


def _params(sem):
    return pltpu.CompilerParams(dimension_semantics=sem, vmem_limit_bytes=VMEM_LIMIT)


def _sigmoid(x):
    return 1.0 / (1.0 + jnp.exp(-x))


class Weight(NamedTuple):
    stacked: jax.Array
    layer: int
    col0: int
    cols: int

    def spec(self, tn):
        assert self.col0 % tn == 0 and self.cols % tn == 0
        first = self.col0 // tn
        return pl.BlockSpec((None, self.stacked.shape[1], tn),
                            lambda i, j, layer=self.layer: (layer, 0, first + j))


def _whole(stacked, layer):
    return Weight(stacked, layer, 0, stacked.shape[2])


def _rmsnorm_kernel(x_ref, g_ref, o_ref):
    x = x_ref[...]
    y = x * lax.rsqrt(jnp.mean(x * x, axis=-1, keepdims=True) + RMS_EPS)
    o_ref[...] = (y * g_ref[...]).astype(o_ref.dtype)


def _rmsnorm(x2, g, *, tm, out_dtype, name):
    M, D = x2.shape
    return pl.pallas_call(
        _rmsnorm_kernel,
        grid=(M // tm,),
        in_specs=[pl.BlockSpec((tm, D), lambda i: (i, 0)), pl.BlockSpec((1, D), lambda i: (0, 0))],
        out_specs=pl.BlockSpec((tm, D), lambda i: (i, 0)),
        out_shape=jax.ShapeDtypeStruct((M, D), out_dtype),
        compiler_params=_params(("parallel",)),
        name=name,
    )(x2, g.reshape(1, D))


def _matmul_kernel(a_ref, w_ref, o_ref, *, layout):
    acc = jnp.dot(a_ref[...], w_ref[...].astype(BF16), preferred_element_type=F32)
    if layout == "rows":
        o_ref[...] = acc.astype(o_ref.dtype)
    else:
        for c in range(acc.shape[1] // HEAD_DIM):
            head = acc[:, c * HEAD_DIM:(c + 1) * HEAD_DIM]
            if layout == "head":
                o_ref[0, c] = head.astype(o_ref.dtype)
            else:
                for r in range(acc.shape[0] // QT):
                    o_ref[0, c, r] = head[r * QT:(r + 1) * QT].T.astype(o_ref.dtype)


def _matmul(a, w, *, tm, tn, out_dtype, layout="rows", batch_seq=None, name):
    M, K = a.shape
    N = w.cols
    if layout == "rows":
        out_shape = jax.ShapeDtypeStruct((M, N), out_dtype)
        out_spec = pl.BlockSpec((tm, tn), lambda i, j: (i, j))
    else:
        B, S = batch_seq
        per_b = S // tm
        if layout == "head":
            out_shape = jax.ShapeDtypeStruct((B, N // HEAD_DIM, S, HEAD_DIM), out_dtype)
            out_spec = pl.BlockSpec((1, tn // HEAD_DIM, tm, HEAD_DIM),
                                    lambda i, j: (i // per_b, j, i % per_b, 0))
        else:
            out_shape = jax.ShapeDtypeStruct((B, N // HEAD_DIM, S // QT, HEAD_DIM, QT), out_dtype)
            out_spec = pl.BlockSpec((1, tn // HEAD_DIM, tm // QT, HEAD_DIM, QT),
                                    lambda i, j: (i // per_b, j, i % per_b, 0, 0))
    return pl.pallas_call(
        functools.partial(_matmul_kernel, layout=layout),
        grid=(M // tm, N // tn),
        in_specs=[pl.BlockSpec((tm, K), lambda i, j: (i, 0)), w.spec(tn)],
        out_specs=out_spec,
        out_shape=out_shape,
        compiler_params=_params(("parallel", "parallel")),
        name=name,
    )(a, w.stacked)


def _t5_bucket(dist):
    n = jnp.maximum(dist, 0)
    max_exact = REL_BUCKETS // 2
    nf = jnp.maximum(n, 1).astype(F32)
    large = max_exact + (jnp.log(nf / max_exact) / math.log(REL_MAX_DIST / max_exact)
                         * (REL_BUCKETS - max_exact)).astype(jnp.int32)
    large = jnp.minimum(large, REL_BUCKETS - 1)
    return jnp.where(n < max_exact, n, large)


def _table_lookup(tab_ref, bucket, col):
    val = jnp.zeros(bucket.shape, F32)
    for b in range(REL_BUCKETS):
        val = jnp.where(bucket == b, tab_ref[b, col], val)
    return val


def _band_bias_kernel(tab_ref, o_ref):
    h = pl.program_id(0)
    dilation = jnp.left_shift(1, 2 * (h // HEADS_PER_GROUP))
    qi = lax.broadcasted_iota(jnp.int32, (BAND, 2 * BAND), 0)
    kj = lax.broadcasted_iota(jnp.int32, (BAND, 2 * BAND), 1)
    delta = qi + BAND - kj
    bias = _table_lookup(tab_ref, _t5_bucket(delta * dilation), h)
    o_ref[0] = jnp.where((delta >= 0) & (delta <= BAND), bias, -jnp.inf)


def _toeplitz_bias_kernel(tab_ref, o_ref):
    h = pl.program_id(0)
    kj = lax.broadcasted_iota(jnp.int32, (QB, QB), 0)
    qi = lax.broadcasted_iota(jnp.int32, (QB, QB), 1)

    def body(d, carry):
        dist = d * QB + qi - kj
        o_ref[0, d] = _table_lookup(tab_ref, _t5_bucket(dist), N_HEADS_A + h) * LOG2E
        return carry

    lax.fori_loop(0, o_ref.shape[1], body, 0)


def _bias_tiles(rel_bias, n_blocks):
    smem = pl.BlockSpec(memory_space=pltpu.SMEM)
    band = pl.pallas_call(
        _band_bias_kernel,
        grid=(N_HEADS_A,),
        in_specs=[smem],
        out_specs=pl.BlockSpec((1, BAND, 2 * BAND), lambda h: (h, 0, 0)),
        out_shape=jax.ShapeDtypeStruct((N_HEADS_A, BAND, 2 * BAND), F32),
        name="band_bias",
    )(rel_bias)
    toep = pl.pallas_call(
        _toeplitz_bias_kernel,
        grid=(N_HEADS_B,),
        in_specs=[smem],
        out_specs=pl.BlockSpec((1, n_blocks, QB, QB), lambda h: (h, 0, 0, 0)),
        out_shape=jax.ShapeDtypeStruct((N_HEADS_B, n_blocks, QB, QB), F32),
        name="toeplitz_bias",
    )(rel_bias)
    return band, toep


def _dilated_kernel(*refs):
    n_g = len(DILATIONS)
    groups = [refs[6 * g:6 * g + 6] for g in range(n_g)]
    y_ref, o_scr, l_scr = refs[6 * n_g:]
    i = pl.program_id(2)
    scale = HEAD_DIM ** -0.5
    col = lax.broadcasted_iota(jnp.int32, (BAND, 2 * BAND), 1)

    for g, d in enumerate(DILATIONS):
        q_ref, k_ref, v_ref, kp_ref, vp_ref, bias_ref = groups[g]
        span = d * BAND
        bias = bias_ref[0]
        bias_first = jnp.where((i > 0) | (col >= BAND), bias, -jnp.inf)
        for r in range(d):
            for n in range(TS // span):
                rows = pl.ds(n * span + r, BAND, stride=d)
                if n == 0:
                    prev = pl.ds(r, BAND, stride=d)
                    k_prev, v_prev, b = kp_ref[0, 0, prev, :], vp_ref[0, 0, prev, :], bias_first
                else:
                    prev = pl.ds((n - 1) * span + r, BAND, stride=d)
                    k_prev, v_prev, b = k_ref[0, 0, prev, :], v_ref[0, 0, prev, :], bias
                kwin = jnp.concatenate([k_prev, k_ref[0, 0, rows, :]], axis=0).astype(BF16)
                vwin = jnp.concatenate([v_prev, v_ref[0, 0, rows, :]], axis=0).astype(BF16)
                q = q_ref[0, 0, rows, :].astype(BF16)
                s = lax.dot_general(q, kwin, (((1,), (1,)), ((), ())), preferred_element_type=F32)
                s = s * scale + b
                m = jnp.max(s, axis=-1, keepdims=True)
                p = jnp.exp(s - m)
                l = jnp.sum(p, axis=-1, keepdims=True)
                o_scr[g, rows, :] = jnp.dot(p.astype(BF16), vwin, preferred_element_type=F32) / l
                l_scr[g, rows, :] = jnp.broadcast_to(m + jnp.log(l), (BAND, HEAD_DIM))

    lses = [l_scr[g] for g in range(n_g)]
    mx = functools.reduce(jnp.maximum, lses)
    es = [jnp.exp(lse - mx) for lse in lses]
    tot = functools.reduce(jnp.add, es)
    y = functools.reduce(jnp.add, [(e / tot) * o_scr[g] for g, e in enumerate(es)])
    y_ref[...] = y.astype(y_ref.dtype)


def _dilated_attention(qkv_a, band_bias):
    B, _, S, _ = qkv_a.shape
    tiles = S // TS
    in_specs, args = [], []
    for g, d in enumerate(DILATIONS):
        span = d * BAND
        per_tile = TS // span
        for base in (0, N_HEADS_A, 2 * N_HEADS_A):
            in_specs.append(pl.BlockSpec(
                (1, 1, TS, HEAD_DIM),
                lambda b, h, i, hd=base + g * HEADS_PER_GROUP: (b, hd + h, i, 0)))
        for base in (N_HEADS_A, 2 * N_HEADS_A):
            in_specs.append(pl.BlockSpec(
                (1, 1, span, HEAD_DIM),
                lambda b, h, i, hd=base + g * HEADS_PER_GROUP, pt=per_tile:
                (b, hd + h, jnp.maximum(i * pt - 1, 0), 0)))
        in_specs.append(pl.BlockSpec((1, BAND, 2 * BAND),
                                     lambda b, h, i, g=g: (g * HEADS_PER_GROUP + h, 0, 0)))
        args += [qkv_a] * 5 + [band_bias]
    return pl.pallas_call(
        _dilated_kernel,
        grid=(B, HEADS_PER_GROUP, tiles),
        in_specs=in_specs,
        out_specs=pl.BlockSpec((TS, HEAD_DIM), lambda b, h, i: (b * tiles + i, h)),
        out_shape=jax.ShapeDtypeStruct((B * S, WIDTH_A_OUT), BF16),
        scratch_shapes=[pltpu.VMEM((len(DILATIONS), TS, HEAD_DIM), F32),
                        pltpu.VMEM((len(DILATIONS), TS, HEAD_DIM), F32)],
        compiler_params=_params(("parallel", "parallel", "parallel")),
        name="dilated_attn",
    )(*args)


def _indexer_kernel(q_ref, k_ref, mask_ref, qh_ref, wb_ref, key_ref, cut_ref, *, n_kb, topk):
    i = pl.program_id(1)
    n_vis = i + 1
    n_chunks = (i + KCHUNK) // KCHUNK
    sub = QB // 8
    for h in range(N_IDX_HEADS):
        qh_ref[h * QB:(h + 1) * QB, :] = q_ref[0, :, h * IDX_DIM:(h + 1) * IDX_DIM].astype(BF16)
    wt = q_ref[0, :, KI_COL:KI_COL + LANES].T * ((N_IDX_HEADS ** -0.5) * (IDX_DIM ** -0.5))
    for h in range(N_IDX_HEADS):
        wb_ref[h] = jnp.broadcast_to(wt[IDX_DIM + h:IDX_DIM + h + 1, :], (8, QB))

    key_pos = (lax.broadcasted_iota(jnp.int32, (sub, 8, QB), 0) * 8
               + lax.broadcasted_iota(jnp.int32, (sub, 8, QB), 1))
    diag_vis = key_pos <= lax.broadcasted_iota(jnp.int32, (sub, 8, QB), 2)

    def score_body(c, carry):
        row0 = pl.multiple_of(c * QT, QT)
        kchunk = k_ref[0, pl.ds(row0, QT), 0:IDX_DIM].astype(BF16)
        accs = [jnp.zeros((sub, 8, QB), F32) for _ in range(KCHUNK)]
        for pair in range(N_IDX_HEADS // 2):
            d = lax.dot_general(kchunk, qh_ref[pair * 2 * QB:(pair + 1) * 2 * QB, :],
                                (((1,), (1,)), ((), ())), preferred_element_type=F32)
            for u in range(2):
                w = wb_ref[2 * pair + u][None]
                for t in range(KCHUNK):
                    tile = d[t * QB:(t + 1) * QB, u * QB:(u + 1) * QB].reshape(sub, 8, QB)
                    accs[t] = accs[t] + jnp.maximum(tile, 0.0) * w
        for t in range(KCHUNK):
            kb = c * KCHUNK + t
            vis = (kb < i) | ((kb == i) & diag_vis)
            bits = pltpu.bitcast(jnp.where(vis, accs[t], -jnp.inf), jnp.int32)
            key_ref[pl.ds(pl.multiple_of(kb * sub, sub), sub)] = bits ^ ((bits >> 31) & 0x7FFFFFFF)
        return carry

    lax.fori_loop(0, n_chunks, score_body, 0)

    def count(pred):
        def count_body(c, cnt):
            for u in range(KCHUNK):
                kb = c * KCHUNK + u
                keys = key_ref[pl.ds(pl.multiple_of(kb * sub, sub), sub)]
                cnt = cnt + jnp.sum(pred(keys, kb).astype(jnp.int32), axis=0)
            return cnt

        cnt = lax.fori_loop(0, n_chunks, count_body, jnp.zeros((8, QB), jnp.int32))
        return jnp.sum(cnt, axis=0, keepdims=True)

    def rows(v):
        return jnp.broadcast_to(v, (8, QB))[None]

    def bit_body(it, state):
        t, cnt_t = state
        cand = t + jnp.left_shift(1, 31 - it)
        cand_b = rows(cand)
        total = count(lambda keys, kb: keys >= cand_b)
        take = total >= topk
        return jnp.where(take, cand, t), jnp.where(take, total, cnt_t)

    n_keys = n_kb * QB
    thr, cnt_thr = lax.fori_loop(
        0, 32, bit_body,
        (jnp.full((1, QB), INT_MIN, jnp.int32), jnp.full((1, QB), n_keys + 1, jnp.int32)))
    thr_b = rows(thr)

    cut_ref[...] = jnp.full((1, QB), n_keys, jnp.int32)

    @pl.when(jnp.max(cnt_thr) > topk)
    def _():
        need = topk - count(lambda keys, kb: keys > thr_b)

        def pos_body(j, u):
            cand = u + jnp.left_shift(1, n_keys.bit_length() - 2 - j)
            cand_b = rows(cand)
            below = count(lambda keys, kb: (keys == thr_b) & (kb * QB + key_pos < cand_b))
            return jnp.where(below < need, cand, u)

        cut_ref[...] = lax.fori_loop(0, n_keys.bit_length() - 1, pos_body,
                                     jnp.zeros((1, QB), jnp.int32))

    cut_b = rows(cut_ref[...])

    def mask_body(kb, carry):
        keys = key_ref[pl.ds(pl.multiple_of(kb * sub, sub), sub)]
        sel = (keys > thr_b) | ((keys == thr_b) & (kb * QB + key_pos <= cut_b))
        sel = sel & ((kb < i) | diag_vis)
        mask_ref[0, 0, kb] = jnp.where(sel, 0.0, NEG).reshape(QB, QB).astype(mask_ref.dtype)
        return carry

    lax.fori_loop(0, n_vis, mask_body, 0)

    def zero_body(kb, carry):
        mask_ref[0, 0, kb] = jnp.full((QB, QB), NEG, mask_ref.dtype)
        return carry

    lax.fori_loop(n_vis, n_kb, zero_body, 0)


def _indexer_mask(idx, topk):
    B, S, _ = idx.shape
    n_kb = S // QB
    return pl.pallas_call(
        functools.partial(_indexer_kernel, n_kb=n_kb, topk=topk),
        grid=(B, n_kb),
        in_specs=[pl.BlockSpec((1, QB, IDX_PAD), lambda b, i: (b, i, 0)),
                  pl.BlockSpec((1, S, LANES), lambda b, i: (b, 0, KI_COL // LANES))],
        out_specs=pl.BlockSpec((1, 1, n_kb, QB, QB), lambda b, i: (b, i, 0, 0, 0)),
        out_shape=jax.ShapeDtypeStruct((B, n_kb, n_kb, QB, QB), BF16),
        scratch_shapes=[pltpu.VMEM((N_IDX_HEADS * QB, IDX_DIM), BF16),
                        pltpu.VMEM((N_IDX_HEADS, 8, QB), F32),
                        pltpu.VMEM((n_kb * QB // 8, 8, QB), jnp.int32),
                        pltpu.VMEM((1, QB), jnp.int32)],
        compiler_params=_params(("parallel", "parallel")),
        name="indexer_topk_mask",
    )(idx, idx)


def _sparse_attn_kernel(q_ref, k_ref, vt_ref, bias_ref, mask_ref, o_ref, m_ref, acc_ref):
    i = pl.program_id(2)
    scale = HEAD_DIM ** -0.5 * LOG2E
    q = q_ref[0, 0]
    m_ref[...] = jnp.full(m_ref.shape, NEG, F32)
    acc_ref[...] = jnp.zeros(acc_ref.shape, F32)
    ones = jnp.ones((ONES_ROWS, QT), BF16)
    q_blocks = QTILE // QB

    def body(c, carry):
        row0 = pl.multiple_of(c * QT, QT)
        s = lax.dot_general(k_ref[0, 0, pl.ds(row0, QT), :], q, (((1,), (1,)), ((), ())),
                            preferred_element_type=F32) * scale
        rows = []
        for t in range(KCHUNK):
            parts = []
            for a in range(q_blocks):
                bias = bias_ref[0, jnp.maximum(i * q_blocks + a - (c * KCHUNK + t), 0)]
                drop = mask_ref[0, a, c * KCHUNK + t].astype(F32)
                parts.append(s[t * QB:(t + 1) * QB, a * QB:(a + 1) * QB] + bias + drop)
            rows.append(jnp.concatenate(parts, axis=1))
        s = jnp.concatenate(rows, axis=0)
        m_old = m_ref[...]
        m_new = jnp.maximum(m_old, jnp.max(s, axis=0, keepdims=True))
        alpha = jnp.exp2(m_old - m_new)
        p = jnp.exp2(s - m_new)
        m_ref[...] = m_new
        v_ones = jnp.concatenate([vt_ref[0, 0, c], ones], axis=0)
        acc_ref[...] = alpha * acc_ref[...] + jnp.dot(v_ones, p.astype(BF16),
                                                      preferred_element_type=F32)
        return carry

    lax.fori_loop(0, (i + 1) * (QTILE // QT), body, 0)
    out = acc_ref[0:HEAD_DIM, :] / acc_ref[HEAD_DIM:HEAD_DIM + 1, :]
    o_ref[0] = out.T.astype(o_ref.dtype)


def _sparse_attention(qk, vt, toep_bias, mask):
    B, _, S, _ = qk.shape
    n_kb = S // QB
    return pl.pallas_call(
        _sparse_attn_kernel,
        grid=(B, N_HEADS_B, S // QTILE),
        in_specs=[pl.BlockSpec((1, 1, QTILE, HEAD_DIM), lambda b, h, i: (b, h, i, 0)),
                  pl.BlockSpec((1, 1, S, HEAD_DIM), lambda b, h, i: (b, N_HEADS_B + h, 0, 0)),
                  pl.BlockSpec((1, 1, S // QT, HEAD_DIM, QT), lambda b, h, i: (b, h, 0, 0, 0)),
                  pl.BlockSpec((1, n_kb, QB, QB), lambda b, h, i: (h, 0, 0, 0)),
                  pl.BlockSpec((1, QTILE // QB, n_kb, QB, QB), lambda b, h, i: (b, i, 0, 0, 0))],
        out_specs=pl.BlockSpec((1, QTILE, HEAD_DIM), lambda b, h, i: (b, i, h)),
        out_shape=jax.ShapeDtypeStruct((B, S, WIDTH_B), BF16),
        scratch_shapes=[pltpu.VMEM((1, QTILE), F32),
                        pltpu.VMEM((HEAD_DIM + ONES_ROWS, QTILE), F32)],
        compiler_params=_params(("parallel", "parallel", "parallel")),
        name="sparse_attn",
    )(qk, qk, vt, toep_bias, mask)


def _merge_kernel(ya_ref, yb_ref, ga_ref, gb_ref, wa_ref, wb_ref, out_ref):
    a = jnp.dot(ya_ref[...], wa_ref[...].astype(BF16), preferred_element_type=F32)
    b = jnp.dot(yb_ref[...], wb_ref[...].astype(BF16), preferred_element_type=F32)
    out_ref[...] = (_sigmoid(ga_ref[...]) * a + _sigmoid(gb_ref[...]) * b).astype(out_ref.dtype)


def _merge(ya, yb, gates, wa, wb, *, tm, tn):
    M = yb.shape[0]
    N = wa.cols
    return pl.pallas_call(
        _merge_kernel,
        grid=(M // tm, N // tn),
        in_specs=[pl.BlockSpec((tm, WIDTH_A_OUT), lambda i, j: (i, 0)),
                  pl.BlockSpec((tm, WIDTH_B), lambda i, j: (i, 0)),
                  pl.BlockSpec((tm, tn), lambda i, j: (i, j)),
                  pl.BlockSpec((tm, tn), lambda i, j: (i, j + N // tn)),
                  wa.spec(tn), wb.spec(tn)],
        out_specs=pl.BlockSpec((tm, tn), lambda i, j: (i, j)),
        out_shape=jax.ShapeDtypeStruct((M, N), BF16),
        compiler_params=_params(("parallel", "parallel")),
        name="gated_merge",
    )(ya, yb, gates, gates, wa.stacked, wb.stacked)


def _matmul_residual_kernel(a_ref, w_ref, r_ref, o_ref):
    o_ref[...] = r_ref[...] + jnp.dot(a_ref[...], w_ref[...].astype(BF16),
                                      preferred_element_type=F32)


def _matmul_residual(a, w, r, *, tm, tn, name):
    M, K = a.shape
    N = w.cols
    return pl.pallas_call(
        _matmul_residual_kernel,
        grid=(M // tm, N // tn),
        in_specs=[pl.BlockSpec((tm, K), lambda i, j: (i, 0)), w.spec(tn),
                  pl.BlockSpec((tm, tn), lambda i, j: (i, j))],
        out_specs=pl.BlockSpec((tm, tn), lambda i, j: (i, j)),
        out_shape=jax.ShapeDtypeStruct((M, N), F32),
        compiler_params=_params(("parallel", "parallel")),
        name=name,
    )(a, w.stacked, r)


def _swiglu_kernel(h_ref, wg_ref, wu_ref, o_ref):
    h = h_ref[...]
    gate = jnp.dot(h, wg_ref[...].astype(BF16), preferred_element_type=F32)
    up = jnp.dot(h, wu_ref[...].astype(BF16), preferred_element_type=F32)
    o_ref[...] = (gate * _sigmoid(gate) * up).astype(o_ref.dtype)


def _swiglu(h, wg, wu, *, tm, tn):
    M, D = h.shape
    N = wg.cols
    return pl.pallas_call(
        _swiglu_kernel,
        grid=(M // tm, N // tn),
        in_specs=[pl.BlockSpec((tm, D), lambda i, j: (i, 0)), wg.spec(tn), wu.spec(tn)],
        out_specs=pl.BlockSpec((tm, tn), lambda i, j: (i, j)),
        out_shape=jax.ShapeDtypeStruct((M, N), BF16),
        compiler_params=_params(("parallel", "parallel")),
        name="swiglu_gate_up",
    )(h, wg.stacked, wu.stacked)


def kernel(x, w_in, w_out_a, w_out_b, w_out, w_ff_gate, w_ff_up, w_ff_down,
           g_mix, g_ffn, g_final, rel_bias):
    B, S, D = x.shape
    depth = w_in.shape[0]
    M = B * S
    assert S % TS == 0 and S % QTILE == 0
    assert w_in.shape[2] == QKV_WIDTH + IDX_WIDTH + 2 * D
    topk = min(TOPK_MAX, S // 4)
    tm, tm_small, tn = 2048, 1024, 512

    band_bias, toep_bias = _bias_tiles(rel_bias, S // QB)
    x2 = x.reshape(M, D)
    h = _rmsnorm(x2, g_mix[0], tm=tm_small, out_dtype=BF16, name="rmsnorm_mix")
    for layer in range(depth):
        wl = w_in[layer]
        w_a = Weight(w_in, layer, 0, 3 * WIDTH_A)
        w_qkb = Weight(w_in, layer, 3 * WIDTH_A, 2 * WIDTH_B)
        w_vb = Weight(w_in, layer, QKV_WIDTH - WIDTH_B, WIDTH_B)
        w_idx = _whole(jnp.pad(wl[:, QKV_WIDTH:QKV_WIDTH + IDX_WIDTH],
                               ((0, 0), (0, IDX_PAD - IDX_WIDTH))).astype(BF16)[None], 0)
        w_gate = _whole(wl[:, QKV_WIDTH + IDX_WIDTH:].astype(BF16)[None], 0)

        qkv_a = _matmul(h, w_a, tm=tm, tn=tn, out_dtype=F32, layout="head",
                        batch_seq=(B, S), name="proj_a")
        qk_b = _matmul(h, w_qkb, tm=tm, tn=tn, out_dtype=BF16, layout="head",
                       batch_seq=(B, S), name="proj_qk_b")
        vt = _matmul(h, w_vb, tm=tm_small, tn=tn, out_dtype=BF16, layout="head_t",
                     batch_seq=(B, S), name="proj_vt")
        idx = _matmul(h, w_idx, tm=tm_small, tn=IDX_PAD, out_dtype=F32, name="proj_idx")
        gates = _matmul(h, w_gate, tm=tm, tn=tn, out_dtype=F32, name="proj_gates")

        y_a = _dilated_attention(qkv_a, band_bias)
        mask = _indexer_mask(idx.reshape(B, S, IDX_PAD), topk)
        y_b = _sparse_attention(qk_b, vt, toep_bias, mask).reshape(M, WIDTH_B)

        merged = _merge(y_a, y_b, gates, _whole(w_out_a, layer), _whole(w_out_b, layer),
                        tm=tm, tn=tn)
        x2 = _matmul_residual(merged, _whole(w_out, layer), x2, tm=tm, tn=tn, name="out_proj")

        h = _rmsnorm(x2, g_ffn[layer], tm=tm_small, out_dtype=BF16, name="rmsnorm_ffn")
        act = _swiglu(h, _whole(w_ff_gate, layer), _whole(w_ff_up, layer), tm=tm, tn=tn)
        x2 = _matmul_residual(act, _whole(w_ff_down, layer), x2, tm=tm_small, tn=tn // 2,
                              name="ffn_down")
        if layer + 1 < depth:
            h = _rmsnorm(x2, g_mix[layer + 1], tm=tm_small, out_dtype=BF16, name="rmsnorm_mix")
    return _rmsnorm(x2, g_final, tm=tm_small, out_dtype=F32, name="final_rmsnorm").reshape(B, S, D)
```

```python
import functools
import math
from typing import NamedTuple

import jax
import jax.numpy as jnp
from jax import lax
from jax.experimental import pallas as pl
from jax.experimental.pallas import tpu as pltpu

F32 = jnp.float32
BF16 = jnp.bfloat16

HEAD_DIM = 128
DILATIONS = (1, 4, 16)
BAND = 128
HEADS_PER_GROUP = 4
N_HEADS_A = 12
N_HEADS_B = 8
WIDTH_A = N_HEADS_A * HEAD_DIM
WIDTH_A_OUT = HEADS_PER_GROUP * HEAD_DIM
WIDTH_B = N_HEADS_B * HEAD_DIM
N_IDX_HEADS = 16
IDX_DIM = 64
TOPK_MAX = 256
REL_BUCKETS = 32
REL_MAX_DIST = 2048
RMS_EPS = 1e-6
QKV_WIDTH = 3 * WIDTH_A + 3 * WIDTH_B
IDX_WIDTH = N_IDX_HEADS * IDX_DIM + IDX_DIM + N_IDX_HEADS
IDX_PAD = 1152
KI_COL = N_IDX_HEADS * IDX_DIM
WI_COL = KI_COL + IDX_DIM
LANES = 128
QB = 128
KCHUNK = 4
QT = KCHUNK * QB
QTILE = 2 * QT
KTILE = 2 * QT
TS = BAND * max(DILATIONS)
NEG = -(2.0 ** 100)
INT_MIN = -(2 ** 31)
LOG2E = math.log2(math.e)
ONES_ROWS = 16
VMEM_LIMIT = 56 * 1024 * 1024


def _params(sem):
    return pltpu.CompilerParams(dimension_semantics=sem, vmem_limit_bytes=VMEM_LIMIT)


def _sigmoid(x):
    return 1.0 / (1.0 + jnp.exp(-x))


class Weight(NamedTuple):
    stacked: jax.Array
    layer: int
    col0: int
    cols: int

    def spec(self, tn):
        assert self.col0 % tn == 0 and self.cols % tn == 0
        first = self.col0 // tn
        return pl.BlockSpec((None, self.stacked.shape[1], tn),
                            lambda i, j, layer=self.layer: (layer, 0, first + j))


def _whole(stacked, layer):
    return Weight(stacked, layer, 0, stacked.shape[2])


def _rmsnorm_kernel(x_ref, g_ref, o_ref):
    x = x_ref[...]
    y = x * lax.rsqrt(jnp.mean(x * x, axis=-1, keepdims=True) + RMS_EPS)
    o_ref[...] = (y * g_ref[...]).astype(o_ref.dtype)


def _rmsnorm(x2, g, *, tm, out_dtype, name):
    M, D = x2.shape
    return pl.pallas_call(
        _rmsnorm_kernel,
        grid=(M // tm,),
        in_specs=[pl.BlockSpec((tm, D), lambda i: (i, 0)), pl.BlockSpec((1, D), lambda i: (0, 0))],
        out_specs=pl.BlockSpec((tm, D), lambda i: (i, 0)),
        out_shape=jax.ShapeDtypeStruct((M, D), out_dtype),
        compiler_params=_params(("parallel",)),
        name=name,
    )(x2, g.reshape(1, D))


def _matmul_kernel(a_ref, w_ref, o_ref, *, layout):
    acc = jnp.dot(a_ref[...], w_ref[...].astype(BF16), preferred_element_type=F32)
    if layout == "rows":
        o_ref[...] = acc.astype(o_ref.dtype)
    else:
        for c in range(acc.shape[1] // HEAD_DIM):
            head = acc[:, c * HEAD_DIM:(c + 1) * HEAD_DIM]
            if layout == "head":
                o_ref[0, c] = head.astype(o_ref.dtype)
            else:
                for r in range(acc.shape[0] // QT):
                    o_ref[0, c, r] = head[r * QT:(r + 1) * QT].T.astype(o_ref.dtype)


def _matmul(a, w, *, tm, tn, out_dtype, layout="rows", batch_seq=None, name):
    M, K = a.shape
    N = w.cols
    if layout == "rows":
        out_shape = jax.ShapeDtypeStruct((M, N), out_dtype)
        out_spec = pl.BlockSpec((tm, tn), lambda i, j: (i, j))
    else:
        B, S = batch_seq
        per_b = S // tm
        if layout == "head":
            out_shape = jax.ShapeDtypeStruct((B, N // HEAD_DIM, S, HEAD_DIM), out_dtype)
            out_spec = pl.BlockSpec((1, tn // HEAD_DIM, tm, HEAD_DIM),
                                    lambda i, j: (i // per_b, j, i % per_b, 0))
        else:
            out_shape = jax.ShapeDtypeStruct((B, N // HEAD_DIM, S // QT, HEAD_DIM, QT), out_dtype)
            out_spec = pl.BlockSpec((1, tn // HEAD_DIM, tm // QT, HEAD_DIM, QT),
                                    lambda i, j: (i // per_b, j, i % per_b, 0, 0))
    return pl.pallas_call(
        functools.partial(_matmul_kernel, layout=layout),
        grid=(M // tm, N // tn),
        in_specs=[pl.BlockSpec((tm, K), lambda i, j: (i, 0)), w.spec(tn)],
        out_specs=out_spec,
        out_shape=out_shape,
        compiler_params=_params(("parallel", "parallel")),
        name=name,
    )(a, w.stacked)


def _t5_bucket(dist):
    n = jnp.maximum(dist, 0)
    max_exact = REL_BUCKETS // 2
    nf = jnp.maximum(n, 1).astype(F32)
    large = max_exact + (jnp.log(nf / max_exact) / math.log(REL_MAX_DIST / max_exact)
                         * (REL_BUCKETS - max_exact)).astype(jnp.int32)
    large = jnp.minimum(large, REL_BUCKETS - 1)
    return jnp.where(n < max_exact, n, large)


def _table_lookup(tab_ref, bucket, col):
    val = jnp.zeros(bucket.shape, F32)
    for b in range(REL_BUCKETS):
        val = jnp.where(bucket == b, tab_ref[b, col], val)
    return val


def _band_bias_kernel(tab_ref, o_ref):
    h = pl.program_id(0)
    dilation = jnp.left_shift(1, 2 * (h // HEADS_PER_GROUP))
    qi = lax.broadcasted_iota(jnp.int32, (BAND, 2 * BAND), 0)
    kj = lax.broadcasted_iota(jnp.int32, (BAND, 2 * BAND), 1)
    delta = qi + BAND - kj
    bias = _table_lookup(tab_ref, _t5_bucket(delta * dilation), h)
    o_ref[0] = jnp.where((delta >= 0) & (delta <= BAND), bias, -jnp.inf)


def _toeplitz_bias_kernel(tab_ref, o_ref):
    h = pl.program_id(0)
    kj = lax.broadcasted_iota(jnp.int32, (QB, QB), 0)
    qi = lax.broadcasted_iota(jnp.int32, (QB, QB), 1)

    def body(d, carry):
        dist = d * QB + qi - kj
        o_ref[0, d] = _table_lookup(tab_ref, _t5_bucket(dist), N_HEADS_A + h) * LOG2E
        return carry

    lax.fori_loop(0, o_ref.shape[1], body, 0)


def _bias_tiles(rel_bias, n_blocks):
    smem = pl.BlockSpec(memory_space=pltpu.SMEM)
    band = pl.pallas_call(
        _band_bias_kernel,
        grid=(N_HEADS_A,),
        in_specs=[smem],
        out_specs=pl.BlockSpec((1, BAND, 2 * BAND), lambda h: (h, 0, 0)),
        out_shape=jax.ShapeDtypeStruct((N_HEADS_A, BAND, 2 * BAND), F32),
        name="band_bias",
    )(rel_bias)
    toep = pl.pallas_call(
        _toeplitz_bias_kernel,
        grid=(N_HEADS_B,),
        in_specs=[smem],
        out_specs=pl.BlockSpec((1, n_blocks, QB, QB), lambda h: (h, 0, 0, 0)),
        out_shape=jax.ShapeDtypeStruct((N_HEADS_B, n_blocks, QB, QB), F32),
        name="toeplitz_bias",
    )(rel_bias)
    return band, toep


def _dilated_kernel(*refs):
    n_g = len(DILATIONS)
    groups = [refs[6 * g:6 * g + 6] for g in range(n_g)]
    y_ref, o_scr, l_scr = refs[6 * n_g:]
    i = pl.program_id(2)
    scale = HEAD_DIM ** -0.5
    col = lax.broadcasted_iota(jnp.int32, (BAND, 2 * BAND), 1)

    for g, d in enumerate(DILATIONS):
        q_ref, k_ref, v_ref, kp_ref, vp_ref, bias_ref = groups[g]
        span = d * BAND
        bias = bias_ref[0]
        bias_first = jnp.where((i > 0) | (col >= BAND), bias, -jnp.inf)
        for r in range(d):
            prev = pl.ds(r, BAND, stride=d)
            k_prev, v_prev = kp_ref[0, 0, prev, :].astype(BF16), vp_ref[0, 0, prev, :].astype(BF16)
            for n in range(TS // span):
                rows = pl.ds(n * span + r, BAND, stride=d)
                b = bias_first if n == 0 else bias
                k_cur, v_cur = k_ref[0, 0, rows, :].astype(BF16), v_ref[0, 0, rows, :].astype(BF16)
                kwin = jnp.concatenate([k_prev, k_cur], axis=0)
                vwin = jnp.concatenate([v_prev, v_cur], axis=0)
                k_prev, v_prev = k_cur, v_cur
                q = q_ref[0, 0, rows, :].astype(BF16)
                s = lax.dot_general(q, kwin, (((1,), (1,)), ((), ())), preferred_element_type=F32)
                s = s * scale + b
                m = jnp.max(s, axis=-1, keepdims=True)
                p = jnp.exp(s - m)
                l = jnp.sum(p, axis=-1, keepdims=True)
                o_scr[g, rows, :] = jnp.dot(p.astype(BF16), vwin, preferred_element_type=F32) / l
                l_scr[g, rows, :] = jnp.broadcast_to(m + jnp.log(l), (BAND, HEAD_DIM))

    lses = [l_scr[g] for g in range(n_g)]
    mx = functools.reduce(jnp.maximum, lses)
    es = [jnp.exp(lse - mx) for lse in lses]
    tot = functools.reduce(jnp.add, es)
    y = functools.reduce(jnp.add, [(e / tot) * o_scr[g] for g, e in enumerate(es)])
    y_ref[...] = y.astype(y_ref.dtype)


def _dilated_attention(qkv_a, band_bias):
    B, _, S, _ = qkv_a.shape
    tiles = S // TS
    in_specs, args = [], []
    for g, d in enumerate(DILATIONS):
        span = d * BAND
        per_tile = TS // span
        for base in (0, N_HEADS_A, 2 * N_HEADS_A):
            in_specs.append(pl.BlockSpec(
                (1, 1, TS, HEAD_DIM),
                lambda b, h, i, hd=base + g * HEADS_PER_GROUP: (b, hd + h, i, 0)))
        for base in (N_HEADS_A, 2 * N_HEADS_A):
            in_specs.append(pl.BlockSpec(
                (1, 1, span, HEAD_DIM),
                lambda b, h, i, hd=base + g * HEADS_PER_GROUP, pt=per_tile:
                (b, hd + h, jnp.maximum(i * pt - 1, 0), 0)))
        in_specs.append(pl.BlockSpec((1, BAND, 2 * BAND),
                                     lambda b, h, i, g=g: (g * HEADS_PER_GROUP + h, 0, 0)))
        args += [qkv_a] * 5 + [band_bias]
    return pl.pallas_call(
        _dilated_kernel,
        grid=(B, HEADS_PER_GROUP, tiles),
        in_specs=in_specs,
        out_specs=pl.BlockSpec((TS, HEAD_DIM), lambda b, h, i: (b * tiles + i, h)),
        out_shape=jax.ShapeDtypeStruct((B * S, WIDTH_A_OUT), BF16),
        scratch_shapes=[pltpu.VMEM((len(DILATIONS), TS, HEAD_DIM), F32),
                        pltpu.VMEM((len(DILATIONS), TS, HEAD_DIM), F32)],
        compiler_params=_params(("parallel", "parallel", "parallel")),
        name="dilated_attn",
    )(*args)


def _indexer_kernel(q_ref, k_ref, mask_ref, qh_ref, wb_ref, key_ref, cut_ref, *, n_kb, topk):
    i = pl.program_id(1)
    n_vis = i + 1
    n_chunks = (i + KCHUNK) // KCHUNK
    sub = QB // 8
    for h in range(N_IDX_HEADS):
        qh_ref[h * QB:(h + 1) * QB, :] = q_ref[0, :, h * IDX_DIM:(h + 1) * IDX_DIM].astype(BF16)
    wt = q_ref[0, :, KI_COL:KI_COL + LANES].T * ((N_IDX_HEADS ** -0.5) * (IDX_DIM ** -0.5))
    for h in range(N_IDX_HEADS):
        wb_ref[h] = jnp.broadcast_to(wt[IDX_DIM + h:IDX_DIM + h + 1, :], (8, QB))

    key_pos = (lax.broadcasted_iota(jnp.int32, (sub, 8, QB), 0) * 8
               + lax.broadcasted_iota(jnp.int32, (sub, 8, QB), 1))
    diag_vis = key_pos <= lax.broadcasted_iota(jnp.int32, (sub, 8, QB), 2)

    def score_body(c, carry):
        row0 = pl.multiple_of(c * QT, QT)
        kchunk = k_ref[0, pl.ds(row0, QT), 0:IDX_DIM].astype(BF16)
        accs = [jnp.zeros((sub, 8, QB), F32) for _ in range(KCHUNK)]
        for pair in range(N_IDX_HEADS // 2):
            d = lax.dot_general(kchunk, qh_ref[pair * 2 * QB:(pair + 1) * 2 * QB, :],
                                (((1,), (1,)), ((), ())), preferred_element_type=F32)
            for u in range(2):
                w = wb_ref[2 * pair + u][None]
                for t in range(KCHUNK):
                    tile = d[t * QB:(t + 1) * QB, u * QB:(u + 1) * QB].reshape(sub, 8, QB)
                    accs[t] = accs[t] + jnp.maximum(tile, 0.0) * w
        for t in range(KCHUNK):
            kb = c * KCHUNK + t
            vis = (kb < i) | ((kb == i) & diag_vis)
            bits = pltpu.bitcast(jnp.where(vis, accs[t], -jnp.inf), jnp.int32)
            key_ref[pl.ds(pl.multiple_of(kb * sub, sub), sub)] = bits ^ ((bits >> 31) & 0x7FFFFFFF)
        return carry

    lax.fori_loop(0, n_chunks, score_body, 0)

    def count(pred):
        def count_body(c, cnt):
            for u in range(KCHUNK):
                kb = c * KCHUNK + u
                keys = key_ref[pl.ds(pl.multiple_of(kb * sub, sub), sub)]
                cnt = cnt + jnp.sum(pred(keys, kb).astype(jnp.int32), axis=0)
            return cnt

        cnt = lax.fori_loop(0, n_chunks, count_body, jnp.zeros((8, QB), jnp.int32))
        return jnp.sum(cnt, axis=0, keepdims=True)

    def rows(v):
        return jnp.broadcast_to(v, (8, QB))[None]

    def bit_body(it, state):
        t, cnt_t = state
        cand = t + jnp.left_shift(1, 31 - it)
        cand_b = rows(cand)
        total = count(lambda keys, kb: keys >= cand_b)
        take = total >= topk
        return jnp.where(take, cand, t), jnp.where(take, total, cnt_t)

    n_keys = n_kb * QB
    thr, cnt_thr = lax.fori_loop(
        0, 32, bit_body,
        (jnp.full((1, QB), INT_MIN, jnp.int32), jnp.full((1, QB), n_keys + 1, jnp.int32)))
    thr_b = rows(thr)

    cut_ref[...] = jnp.full((1, QB), n_keys, jnp.int32)

    @pl.when(jnp.max(cnt_thr) > topk)
    def _():
        need = topk - count(lambda keys, kb: keys > thr_b)

        def pos_body(j, u):
            cand = u + jnp.left_shift(1, n_keys.bit_length() - 2 - j)
            cand_b = rows(cand)
            below = count(lambda keys, kb: (keys == thr_b) & (kb * QB + key_pos < cand_b))
            return jnp.where(below < need, cand, u)

        cut_ref[...] = lax.fori_loop(0, n_keys.bit_length() - 1, pos_body,
                                     jnp.zeros((1, QB), jnp.int32))

    cut_b = rows(cut_ref[...])

    def mask_body(kb, carry):
        keys = key_ref[pl.ds(pl.multiple_of(kb * sub, sub), sub)]
        sel = (keys > thr_b) | ((keys == thr_b) & (kb * QB + key_pos <= cut_b))
        sel = sel & ((kb < i) | diag_vis)
        mask_ref[0, 0, kb] = jnp.where(sel, 0.0, NEG).reshape(QB, QB).astype(mask_ref.dtype)
        return carry

    lax.fori_loop(0, n_vis, mask_body, 0)

    def zero_body(kb, carry):
        mask_ref[0, 0, kb] = jnp.full((QB, QB), NEG, mask_ref.dtype)
        return carry

    lax.fori_loop(n_vis, n_kb, zero_body, 0)


def _indexer_mask(idx, topk):
    B, S, _ = idx.shape
    n_kb = S // QB
    return pl.pallas_call(
        functools.partial(_indexer_kernel, n_kb=n_kb, topk=topk),
        grid=(B, n_kb),
        in_specs=[pl.BlockSpec((1, QB, IDX_PAD), lambda b, i: (b, i, 0)),
                  pl.BlockSpec((1, S, LANES), lambda b, i: (b, 0, KI_COL // LANES))],
        out_specs=pl.BlockSpec((1, 1, n_kb, QB, QB), lambda b, i: (b, i, 0, 0, 0)),
        out_shape=jax.ShapeDtypeStruct((B, n_kb, n_kb, QB, QB), BF16),
        scratch_shapes=[pltpu.VMEM((N_IDX_HEADS * QB, IDX_DIM), BF16),
                        pltpu.VMEM((N_IDX_HEADS, 8, QB), F32),
                        pltpu.VMEM((n_kb * QB // 8, 8, QB), jnp.int32),
                        pltpu.VMEM((1, QB), jnp.int32)],
        compiler_params=_params(("parallel", "parallel")),
        name="indexer_topk_mask",
    )(idx, idx)


def _sparse_attn_kernel(q_ref, k_ref, vt_ref, bias_ref, mask_ref, o_ref, m_ref, acc_ref):
    i = pl.program_id(2)
    scale = HEAD_DIM ** -0.5 * LOG2E
    q = q_ref[0, 0]
    m_ref[...] = jnp.full(m_ref.shape, NEG, F32)
    acc_ref[...] = jnp.zeros(acc_ref.shape, F32)
    ones = jnp.ones((ONES_ROWS, KTILE), BF16)
    q_blocks = QTILE // QB
    k_blocks = KTILE // QB

    def body(c, carry):
        row0 = pl.multiple_of(c * KTILE, KTILE)
        s = lax.dot_general(k_ref[0, 0, pl.ds(row0, KTILE), :], q, (((1,), (1,)), ((), ())),
                            preferred_element_type=F32) * scale
        rows = []
        for t in range(k_blocks):
            parts = []
            for a in range(q_blocks):
                bias = bias_ref[0, jnp.maximum(i * q_blocks + a - (c * k_blocks + t), 0)]
                drop = mask_ref[0, a, c * k_blocks + t].astype(F32)
                parts.append(s[t * QB:(t + 1) * QB, a * QB:(a + 1) * QB] + bias + drop)
            rows.append(jnp.concatenate(parts, axis=1))
        s = jnp.concatenate(rows, axis=0)
        m_old = m_ref[...]
        m_new = jnp.maximum(m_old, jnp.max(s, axis=0, keepdims=True))
        alpha = jnp.exp2(m_old - m_new)
        p = jnp.exp2(s - m_new)
        m_ref[...] = m_new
        vt = [vt_ref[0, 0, c * (KTILE // QT) + j] for j in range(KTILE // QT)]
        v_ones = jnp.concatenate([jnp.concatenate(vt, axis=1), ones], axis=0)
        acc_ref[...] = alpha * acc_ref[...] + jnp.dot(v_ones, p.astype(BF16),
                                                      preferred_element_type=F32)
        return carry

    lax.fori_loop(0, (i + 1) * (QTILE // KTILE), body, 0)
    out = acc_ref[0:HEAD_DIM, :] / acc_ref[HEAD_DIM:HEAD_DIM + 1, :]
    o_ref[0] = out.T.astype(o_ref.dtype)


def _sparse_attention(qk, vt, toep_bias, mask):
    B, _, S, _ = qk.shape
    n_kb = S // QB
    return pl.pallas_call(
        _sparse_attn_kernel,
        grid=(B, N_HEADS_B, S // QTILE),
        in_specs=[pl.BlockSpec((1, 1, QTILE, HEAD_DIM), lambda b, h, i: (b, h, i, 0)),
                  pl.BlockSpec((1, 1, S, HEAD_DIM), lambda b, h, i: (b, N_HEADS_B + h, 0, 0)),
                  pl.BlockSpec((1, 1, S // QT, HEAD_DIM, QT), lambda b, h, i: (b, h, 0, 0, 0)),
                  pl.BlockSpec((1, n_kb, QB, QB), lambda b, h, i: (h, 0, 0, 0)),
                  pl.BlockSpec((1, QTILE // QB, n_kb, QB, QB), lambda b, h, i: (b, i, 0, 0, 0))],
        out_specs=pl.BlockSpec((1, QTILE, HEAD_DIM), lambda b, h, i: (b, i, h)),
        out_shape=jax.ShapeDtypeStruct((B, S, WIDTH_B), BF16),
        scratch_shapes=[pltpu.VMEM((1, QTILE), F32),
                        pltpu.VMEM((HEAD_DIM + ONES_ROWS, QTILE), F32)],
        compiler_params=_params(("parallel", "parallel", "parallel")),
        name="sparse_attn",
    )(qk, qk, vt, toep_bias, mask)


def _merge_kernel(ya_ref, yb_ref, ga_ref, gb_ref, wa_ref, wb_ref, out_ref):
    a = jnp.dot(ya_ref[...], wa_ref[...].astype(BF16), preferred_element_type=F32)
    b = jnp.dot(yb_ref[...], wb_ref[...].astype(BF16), preferred_element_type=F32)
    out_ref[...] = (_sigmoid(ga_ref[...]) * a + _sigmoid(gb_ref[...]) * b).astype(out_ref.dtype)


def _merge(ya, yb, gates, wa, wb, *, tm, tn):
    M = yb.shape[0]
    N = wa.cols
    return pl.pallas_call(
        _merge_kernel,
        grid=(M // tm, N // tn),
        in_specs=[pl.BlockSpec((tm, WIDTH_A_OUT), lambda i, j: (i, 0)),
                  pl.BlockSpec((tm, WIDTH_B), lambda i, j: (i, 0)),
                  pl.BlockSpec((tm, tn), lambda i, j: (i, j)),
                  pl.BlockSpec((tm, tn), lambda i, j: (i, j + N // tn)),
                  wa.spec(tn), wb.spec(tn)],
        out_specs=pl.BlockSpec((tm, tn), lambda i, j: (i, j)),
        out_shape=jax.ShapeDtypeStruct((M, N), BF16),
        compiler_params=_params(("parallel", "parallel")),
        name="gated_merge",
    )(ya, yb, gates, gates, wa.stacked, wb.stacked)


def _matmul_residual_kernel(a_ref, w_ref, r_ref, o_ref):
    o_ref[...] = r_ref[...] + jnp.dot(a_ref[...], w_ref[...].astype(BF16),
                                      preferred_element_type=F32)


def _matmul_residual(a, w, r, *, tm, tn, name):
    M, K = a.shape
    N = w.cols
    return pl.pallas_call(
        _matmul_residual_kernel,
        grid=(M // tm, N // tn),
        in_specs=[pl.BlockSpec((tm, K), lambda i, j: (i, 0)), w.spec(tn),
                  pl.BlockSpec((tm, tn), lambda i, j: (i, j))],
        out_specs=pl.BlockSpec((tm, tn), lambda i, j: (i, j)),
        out_shape=jax.ShapeDtypeStruct((M, N), F32),
        compiler_params=_params(("parallel", "parallel")),
        name=name,
    )(a, w.stacked, r)


def _swiglu_kernel(h_ref, wg_ref, wu_ref, o_ref):
    h = h_ref[...]
    gate = jnp.dot(h, wg_ref[...].astype(BF16), preferred_element_type=F32)
    up = jnp.dot(h, wu_ref[...].astype(BF16), preferred_element_type=F32)
    o_ref[...] = (gate * _sigmoid(gate) * up).astype(o_ref.dtype)


def _swiglu(h, wg, wu, *, tm, tn):
    M, D = h.shape
    N = wg.cols
    return pl.pallas_call(
        _swiglu_kernel,
        grid=(M // tm, N // tn),
        in_specs=[pl.BlockSpec((tm, D), lambda i, j: (i, 0)), wg.spec(tn), wu.spec(tn)],
        out_specs=pl.BlockSpec((tm, tn), lambda i, j: (i, j)),
        out_shape=jax.ShapeDtypeStruct((M, N), BF16),
        compiler_params=_params(("parallel", "parallel")),
        name="swiglu_gate_up",
    )(h, wg.stacked, wu.stacked)


def kernel(x, w_in, w_out_a, w_out_b, w_out, w_ff_gate, w_ff_up, w_ff_down,
           g_mix, g_ffn, g_final, rel_bias):
    B, S, D = x.shape
    depth = w_in.shape[0]
    M = B * S
    assert S % TS == 0 and S % QTILE == 0
    assert w_in.shape[2] == QKV_WIDTH + IDX_WIDTH + 2 * D
    topk = min(TOPK_MAX, S // 4)
    tm, tm_small, tn = 2048, 1024, 512

    band_bias, toep_bias = _bias_tiles(rel_bias, S // QB)
    x2 = x.reshape(M, D)
    h = _rmsnorm(x2, g_mix[0], tm=tm_small, out_dtype=BF16, name="rmsnorm_mix")
    for layer in range(depth):
        wl = w_in[layer]
        w_qkv = wl[:, :QKV_WIDTH].astype(BF16)[None]
        w_a = Weight(w_qkv, 0, 0, 3 * WIDTH_A)
        w_qkb = Weight(w_qkv, 0, 3 * WIDTH_A, 2 * WIDTH_B)
        w_vb = Weight(w_qkv, 0, QKV_WIDTH - WIDTH_B, WIDTH_B)
        w_idx = _whole(jnp.pad(wl[:, QKV_WIDTH:QKV_WIDTH + IDX_WIDTH],
                               ((0, 0), (0, IDX_PAD - IDX_WIDTH))).astype(BF16)[None], 0)
        w_gate = _whole(wl[:, QKV_WIDTH + IDX_WIDTH:].astype(BF16)[None], 0)

        qkv_a = _matmul(h, w_a, tm=tm, tn=tn, out_dtype=F32, layout="head",
                        batch_seq=(B, S), name="proj_a")
        qk_b = _matmul(h, w_qkb, tm=tm, tn=tn, out_dtype=BF16, layout="head",
                       batch_seq=(B, S), name="proj_qk_b")
        vt = _matmul(h, w_vb, tm=tm_small, tn=tn, out_dtype=BF16, layout="head_t",
                     batch_seq=(B, S), name="proj_vt")
        idx = _matmul(h, w_idx, tm=tm_small, tn=IDX_PAD, out_dtype=F32, name="proj_idx")
        gates = _matmul(h, w_gate, tm=tm, tn=tn, out_dtype=F32, name="proj_gates")

        y_a = _dilated_attention(qkv_a, band_bias)
        mask = _indexer_mask(idx.reshape(B, S, IDX_PAD), topk)
        y_b = _sparse_attention(qk_b, vt, toep_bias, mask).reshape(M, WIDTH_B)

        merged = _merge(y_a, y_b, gates, _whole(w_out_a, layer), _whole(w_out_b, layer),
                        tm=tm, tn=tn)
        x2 = _matmul_residual(merged, _whole(w_out, layer), x2, tm=tm, tn=tn, name="out_proj")

        h = _rmsnorm(x2, g_ffn[layer], tm=tm_small, out_dtype=BF16, name="rmsnorm_ffn")
        act = _swiglu(h, _whole(w_ff_gate, layer), _whole(w_ff_up, layer), tm=tm, tn=tn)
        x2 = _matmul_residual(act, _whole(w_ff_down[layer].astype(BF16)[None], 0), x2,
                              tm=tm_small, tn=tn, name="ffn_down")
        if layer + 1 < depth:
            h = _rmsnorm(x2, g_mix[layer + 1], tm=tm_small, out_dtype=BF16, name="rmsnorm_mix")
    return _rmsnorm(x2, g_final, tm=tm_small, out_dtype=F32, name="final_rmsnorm").reshape(B, S, D)
```

```python
import functools
import math
from typing import NamedTuple

import jax
import jax.numpy as jnp
from jax import lax
from jax.experimental import pallas as pl
from jax.experimental.pallas import tpu as pltpu

F32 = jnp.float32
BF16 = jnp.bfloat16

HEAD_DIM = 128
DILATIONS = (1, 4, 16)
BAND = 128
HEADS_PER_GROUP = 4
N_HEADS_A = 12
N_HEADS_B = 8
WIDTH_A = N_HEADS_A * HEAD_DIM
WIDTH_A_OUT = HEADS_PER_GROUP * HEAD_DIM
WIDTH_B = N_HEADS_B * HEAD_DIM
N_IDX_HEADS = 16
IDX_DIM = 64
TOPK_MAX = 256
REL_BUCKETS = 32
REL_MAX_DIST = 2048
RMS_EPS = 1e-6
QKV_WIDTH = 3 * WIDTH_A + 3 * WIDTH_B
IDX_WIDTH = N_IDX_HEADS * IDX_DIM + IDX_DIM + N_IDX_HEADS
IDX_PAD = 1152
KI_COL = N_IDX_HEADS * IDX_DIM
WI_COL = KI_COL + IDX_DIM
LANES = 128
QB = 128
KCHUNK = 4
QT = KCHUNK * QB
QTILE = 2 * QT
KTILE = 2 * QT
TS = BAND * max(DILATIONS)
NEG = -(2.0 ** 100)
INT_MIN = -(2 ** 31)
LOG2E = math.log2(math.e)
ONES_ROWS = 16
VMEM_LIMIT = 56 * 1024 * 1024


def _params(sem):
    return pltpu.CompilerParams(dimension_semantics=sem, vmem_limit_bytes=VMEM_LIMIT)


def _sigmoid(x):
    return 1.0 / (1.0 + jnp.exp(-x))


class Weight(NamedTuple):
    stacked: jax.Array
    layer: int
    col0: int
    cols: int

    def spec(self, tn):
        assert self.col0 % tn == 0 and self.cols % tn == 0
        first = self.col0 // tn
        return pl.BlockSpec((None, self.stacked.shape[1], tn),
                            lambda i, j, layer=self.layer: (layer, 0, first + j))


def _whole(stacked, layer):
    return Weight(stacked, layer, 0, stacked.shape[2])


def _rmsnorm_kernel(x_ref, g_ref, o_ref):
    x = x_ref[...]
    y = x * lax.rsqrt(jnp.mean(x * x, axis=-1, keepdims=True) + RMS_EPS)
    o_ref[...] = (y * g_ref[...]).astype(o_ref.dtype)


def _rmsnorm(x2, g, *, tm, out_dtype, name):
    M, D = x2.shape
    return pl.pallas_call(
        _rmsnorm_kernel,
        grid=(M // tm,),
        in_specs=[pl.BlockSpec((tm, D), lambda i: (i, 0)), pl.BlockSpec((1, D), lambda i: (0, 0))],
        out_specs=pl.BlockSpec((tm, D), lambda i: (i, 0)),
        out_shape=jax.ShapeDtypeStruct((M, D), out_dtype),
        compiler_params=_params(("parallel",)),
        name=name,
    )(x2, g.reshape(1, D))


def _matmul_kernel(a_ref, w_ref, o_ref, *, layout):
    acc = jnp.dot(a_ref[...], w_ref[...].astype(BF16), preferred_element_type=F32)
    if layout == "rows":
        o_ref[...] = acc.astype(o_ref.dtype)
    elif layout == "sigmoid":
        o_ref[...] = _sigmoid(acc).astype(o_ref.dtype)
    else:
        for c in range(acc.shape[1] // HEAD_DIM):
            head = acc[:, c * HEAD_DIM:(c + 1) * HEAD_DIM]
            if layout == "head":
                o_ref[0, c] = head.astype(o_ref.dtype)
            else:
                for r in range(acc.shape[0] // QT):
                    o_ref[0, c, r] = head[r * QT:(r + 1) * QT].T.astype(o_ref.dtype)


def _matmul(a, w, *, tm, tn, out_dtype, layout="rows", batch_seq=None, name):
    M, K = a.shape
    N = w.cols
    if layout in ("rows", "sigmoid"):
        out_shape = jax.ShapeDtypeStruct((M, N), out_dtype)
        out_spec = pl.BlockSpec((tm, tn), lambda i, j: (i, j))
    else:
        B, S = batch_seq
        per_b = S // tm
        if layout == "head":
            out_shape = jax.ShapeDtypeStruct((B, N // HEAD_DIM, S, HEAD_DIM), out_dtype)
            out_spec = pl.BlockSpec((1, tn // HEAD_DIM, tm, HEAD_DIM),
                                    lambda i, j: (i // per_b, j, i % per_b, 0))
        else:
            out_shape = jax.ShapeDtypeStruct((B, N // HEAD_DIM, S // QT, HEAD_DIM, QT), out_dtype)
            out_spec = pl.BlockSpec((1, tn // HEAD_DIM, tm // QT, HEAD_DIM, QT),
                                    lambda i, j: (i // per_b, j, i % per_b, 0, 0))
    return pl.pallas_call(
        functools.partial(_matmul_kernel, layout=layout),
        grid=(M // tm, N // tn),
        in_specs=[pl.BlockSpec((tm, K), lambda i, j: (i, 0)), w.spec(tn)],
        out_specs=out_spec,
        out_shape=out_shape,
        compiler_params=_params(("parallel", "parallel")),
        name=name,
    )(a, w.stacked)


def _t5_bucket(dist):
    n = jnp.maximum(dist, 0)
    max_exact = REL_BUCKETS // 2
    nf = jnp.maximum(n, 1).astype(F32)
    large = max_exact + (jnp.log(nf / max_exact) / math.log(REL_MAX_DIST / max_exact)
                         * (REL_BUCKETS - max_exact)).astype(jnp.int32)
    large = jnp.minimum(large, REL_BUCKETS - 1)
    return jnp.where(n < max_exact, n, large)


def _table_lookup(tab_ref, bucket, col):
    val = jnp.zeros(bucket.shape, F32)
    for b in range(REL_BUCKETS):
        val = jnp.where(bucket == b, tab_ref[b, col], val)
    return val


def _band_bias_kernel(tab_ref, o_ref):
    h = pl.program_id(0)
    dilation = jnp.left_shift(1, 2 * (h // HEADS_PER_GROUP))
    qi = lax.broadcasted_iota(jnp.int32, (BAND, 2 * BAND), 0)
    kj = lax.broadcasted_iota(jnp.int32, (BAND, 2 * BAND), 1)
    delta = qi + BAND - kj
    bias = _table_lookup(tab_ref, _t5_bucket(delta * dilation), h)
    o_ref[0] = jnp.where((delta >= 0) & (delta <= BAND), bias, -jnp.inf)


def _toeplitz_bias_kernel(tab_ref, o_ref):
    h = pl.program_id(0)
    kj = lax.broadcasted_iota(jnp.int32, (QB, QB), 0)
    qi = lax.broadcasted_iota(jnp.int32, (QB, QB), 1)

    def body(d, carry):
        dist = d * QB + qi - kj
        o_ref[0, d] = _table_lookup(tab_ref, _t5_bucket(dist), N_HEADS_A + h) * LOG2E
        return carry

    lax.fori_loop(0, o_ref.shape[1], body, 0)


def _bias_tiles(rel_bias, n_blocks):
    smem = pl.BlockSpec(memory_space=pltpu.SMEM)
    band = pl.pallas_call(
        _band_bias_kernel,
        grid=(N_HEADS_A,),
        in_specs=[smem],
        out_specs=pl.BlockSpec((1, BAND, 2 * BAND), lambda h: (h, 0, 0)),
        out_shape=jax.ShapeDtypeStruct((N_HEADS_A, BAND, 2 * BAND), F32),
        name="band_bias",
    )(rel_bias)
    toep = pl.pallas_call(
        _toeplitz_bias_kernel,
        grid=(N_HEADS_B,),
        in_specs=[smem],
        out_specs=pl.BlockSpec((1, n_blocks, QB, QB), lambda h: (h, 0, 0, 0)),
        out_shape=jax.ShapeDtypeStruct((N_HEADS_B, n_blocks, QB, QB), F32),
        name="toeplitz_bias",
    )(rel_bias)
    return band, toep


def _dilated_kernel(*refs):
    n_g = len(DILATIONS)
    groups = [refs[6 * g:6 * g + 6] for g in range(n_g)]
    y_ref, o_scr, l_scr = refs[6 * n_g:]
    i = pl.program_id(2)
    scale = HEAD_DIM ** -0.5
    col = lax.broadcasted_iota(jnp.int32, (BAND, 2 * BAND), 1)

    for g, d in enumerate(DILATIONS):
        q_ref, k_ref, v_ref, kp_ref, vp_ref, bias_ref = groups[g]
        span = d * BAND
        bias = bias_ref[0]
        bias_first = jnp.where((i > 0) | (col >= BAND), bias, -jnp.inf)
        for r in range(d):
            prev = pl.ds(r, BAND, stride=d)
            k_prev, v_prev = kp_ref[0, 0, prev, :].astype(BF16), vp_ref[0, 0, prev, :].astype(BF16)
            for n in range(TS // span):
                rows = pl.ds(n * span + r, BAND, stride=d)
                b = bias_first if n == 0 else bias
                k_cur, v_cur = k_ref[0, 0, rows, :].astype(BF16), v_ref[0, 0, rows, :].astype(BF16)
                kwin = jnp.concatenate([k_prev, k_cur], axis=0)
                vwin = jnp.concatenate([v_prev, v_cur], axis=0)
                k_prev, v_prev = k_cur, v_cur
                q = q_ref[0, 0, rows, :].astype(BF16)
                s = lax.dot_general(q, kwin, (((1,), (1,)), ((), ())), preferred_element_type=F32)
                s = s * scale + b
                m = jnp.max(s, axis=-1, keepdims=True)
                p = jnp.exp(s - m)
                l = jnp.sum(p, axis=-1, keepdims=True)
                o_scr[g, rows, :] = jnp.dot(p.astype(BF16), vwin, preferred_element_type=F32) / l
                l_scr[g, rows, :] = jnp.broadcast_to(m + jnp.log(l), (BAND, HEAD_DIM))

    lses = [l_scr[g] for g in range(n_g)]
    mx = functools.reduce(jnp.maximum, lses)
    es = [jnp.exp(lse - mx) for lse in lses]
    tot = functools.reduce(jnp.add, es)
    y = functools.reduce(jnp.add, [(e / tot) * o_scr[g] for g, e in enumerate(es)])
    y_ref[...] = y.astype(y_ref.dtype)


def _dilated_attention(qkv_a, band_bias):
    B, _, S, _ = qkv_a.shape
    tiles = S // TS
    in_specs, args = [], []
    for g, d in enumerate(DILATIONS):
        span = d * BAND
        per_tile = TS // span
        for base in (0, N_HEADS_A, 2 * N_HEADS_A):
            in_specs.append(pl.BlockSpec(
                (1, 1, TS, HEAD_DIM),
                lambda b, h, i, hd=base + g * HEADS_PER_GROUP: (b, hd + h, i, 0)))
        for base in (N_HEADS_A, 2 * N_HEADS_A):
            in_specs.append(pl.BlockSpec(
                (1, 1, span, HEAD_DIM),
                lambda b, h, i, hd=base + g * HEADS_PER_GROUP, pt=per_tile:
                (b, hd + h, jnp.maximum(i * pt - 1, 0), 0)))
        in_specs.append(pl.BlockSpec((1, BAND, 2 * BAND),
                                     lambda b, h, i, g=g: (g * HEADS_PER_GROUP + h, 0, 0)))
        args += [qkv_a] * 5 + [band_bias]
    return pl.pallas_call(
        _dilated_kernel,
        grid=(B, HEADS_PER_GROUP, tiles),
        in_specs=in_specs,
        out_specs=pl.BlockSpec((TS, HEAD_DIM), lambda b, h, i: (b * tiles + i, h)),
        out_shape=jax.ShapeDtypeStruct((B * S, WIDTH_A_OUT), BF16),
        scratch_shapes=[pltpu.VMEM((len(DILATIONS), TS, HEAD_DIM), F32),
                        pltpu.VMEM((len(DILATIONS), TS, HEAD_DIM), F32)],
        compiler_params=_params(("parallel", "parallel", "parallel")),
        name="dilated_attn",
    )(*args)


def _indexer_kernel(q_ref, k_ref, mask_ref, qh_ref, wb_ref, key_ref, cut_ref, *, n_kb, topk):
    i = pl.program_id(1)
    n_vis = i + 1
    n_chunks = (i + KCHUNK) // KCHUNK
    sub = QB // 8
    for h in range(N_IDX_HEADS):
        qh_ref[h * QB:(h + 1) * QB, :] = q_ref[0, :, h * IDX_DIM:(h + 1) * IDX_DIM].astype(BF16)
    wt = q_ref[0, :, KI_COL:KI_COL + LANES].T * ((N_IDX_HEADS ** -0.5) * (IDX_DIM ** -0.5))
    for h in range(N_IDX_HEADS):
        wb_ref[h] = jnp.broadcast_to(wt[IDX_DIM + h:IDX_DIM + h + 1, :], (8, QB))

    key_pos = (lax.broadcasted_iota(jnp.int32, (sub, 8, QB), 0) * 8
               + lax.broadcasted_iota(jnp.int32, (sub, 8, QB), 1))
    diag_vis = key_pos <= lax.broadcasted_iota(jnp.int32, (sub, 8, QB), 2)

    def score_body(c, carry):
        row0 = pl.multiple_of(c * QT, QT)
        kchunk = k_ref[0, pl.ds(row0, QT), 0:IDX_DIM].astype(BF16)
        accs = [jnp.zeros((sub, 8, QB), F32) for _ in range(KCHUNK)]
        for pair in range(N_IDX_HEADS // 2):
            d = lax.dot_general(kchunk, qh_ref[pair * 2 * QB:(pair + 1) * 2 * QB, :],
                                (((1,), (1,)), ((), ())), preferred_element_type=F32)
            for u in range(2):
                w = wb_ref[2 * pair + u][None]
                for t in range(KCHUNK):
                    tile = d[t * QB:(t + 1) * QB, u * QB:(u + 1) * QB].reshape(sub, 8, QB)
                    accs[t] = accs[t] + jnp.maximum(tile, 0.0) * w
        for t in range(KCHUNK):
            kb = c * KCHUNK + t
            vis = (kb < i) | ((kb == i) & diag_vis)
            bits = pltpu.bitcast(jnp.where(vis, accs[t], -jnp.inf), jnp.int32)
            key_ref[pl.ds(pl.multiple_of(kb * sub, sub), sub)] = bits ^ ((bits >> 31) & 0x7FFFFFFF)
        return carry

    lax.fori_loop(0, n_chunks, score_body, 0)

    def count(pred):
        def count_body(c, cnt):
            for u in range(KCHUNK):
                kb = c * KCHUNK + u
                keys = key_ref[pl.ds(pl.multiple_of(kb * sub, sub), sub)]
                cnt = cnt + jnp.sum(pred(keys, kb).astype(jnp.int32), axis=0)
            return cnt

        cnt = lax.fori_loop(0, n_chunks, count_body, jnp.zeros((8, QB), jnp.int32))
        return jnp.sum(cnt, axis=0, keepdims=True)

    def rows(v):
        return jnp.broadcast_to(v, (8, QB))[None]

    def bit_body(it, state):
        t, cnt_t = state
        cand = t + jnp.left_shift(1, 31 - it)
        cand_b = rows(cand)
        total = count(lambda keys, kb: keys >= cand_b)
        take = total >= topk
        return jnp.where(take, cand, t), jnp.where(take, total, cnt_t)

    n_keys = n_kb * QB
    thr, cnt_thr = lax.fori_loop(
        0, 32, bit_body,
        (jnp.full((1, QB), INT_MIN, jnp.int32), jnp.full((1, QB), n_keys + 1, jnp.int32)))
    thr_b = rows(thr)

    cut_ref[...] = jnp.full((1, QB), n_keys, jnp.int32)

    @pl.when(jnp.max(cnt_thr) > topk)
    def _():
        need = topk - count(lambda keys, kb: keys > thr_b)

        def pos_body(j, u):
            cand = u + jnp.left_shift(1, n_keys.bit_length() - 2 - j)
            cand_b = rows(cand)
            below = count(lambda keys, kb: (keys == thr_b) & (kb * QB + key_pos < cand_b))
            return jnp.where(below < need, cand, u)

        cut_ref[...] = lax.fori_loop(0, n_keys.bit_length() - 1, pos_body,
                                     jnp.zeros((1, QB), jnp.int32))

    cut_b = rows(cut_ref[...])

    def mask_body(kb, carry):
        keys = key_ref[pl.ds(pl.multiple_of(kb * sub, sub), sub)]
        sel = (keys > thr_b) | ((keys == thr_b) & (kb * QB + key_pos <= cut_b))
        sel = sel & ((kb < i) | diag_vis)
        mask_ref[0, 0, kb] = jnp.where(sel, 0.0, NEG).reshape(QB, QB).astype(mask_ref.dtype)
        return carry

    lax.fori_loop(0, n_vis, mask_body, 0)

    def zero_body(kb, carry):
        mask_ref[0, 0, kb] = jnp.full((QB, QB), NEG, mask_ref.dtype)
        return carry

    lax.fori_loop(n_vis, n_kb, zero_body, 0)


def _indexer_mask(idx, topk):
    B, S, _ = idx.shape
    n_kb = S // QB
    return pl.pallas_call(
        functools.partial(_indexer_kernel, n_kb=n_kb, topk=topk),
        grid=(B, n_kb),
        in_specs=[pl.BlockSpec((1, QB, IDX_PAD), lambda b, i: (b, i, 0)),
                  pl.BlockSpec((1, S, LANES), lambda b, i: (b, 0, KI_COL // LANES))],
        out_specs=pl.BlockSpec((1, 1, n_kb, QB, QB), lambda b, i: (b, i, 0, 0, 0)),
        out_shape=jax.ShapeDtypeStruct((B, n_kb, n_kb, QB, QB), BF16),
        scratch_shapes=[pltpu.VMEM((N_IDX_HEADS * QB, IDX_DIM), BF16),
                        pltpu.VMEM((N_IDX_HEADS, 8, QB), F32),
                        pltpu.VMEM((n_kb * QB // 8, 8, QB), jnp.int32),
                        pltpu.VMEM((1, QB), jnp.int32)],
        compiler_params=_params(("parallel", "parallel")),
        name="indexer_topk_mask",
    )(idx, idx)


def _sparse_attn_kernel(q_ref, k_ref, vt_ref, bias_ref, mask_ref, o_ref, m_ref, acc_ref):
    i = pl.program_id(2)
    scale = HEAD_DIM ** -0.5 * LOG2E
    q = q_ref[0, 0]
    m_ref[...] = jnp.full(m_ref.shape, NEG, F32)
    acc_ref[...] = jnp.zeros(acc_ref.shape, F32)
    ones = jnp.ones((ONES_ROWS, KTILE), BF16)
    q_blocks = QTILE // QB
    k_blocks = KTILE // QB

    def body(c, carry):
        row0 = pl.multiple_of(c * KTILE, KTILE)
        s = lax.dot_general(k_ref[0, 0, pl.ds(row0, KTILE), :], q, (((1,), (1,)), ((), ())),
                            preferred_element_type=F32) * scale
        rows = []
        for t in range(k_blocks):
            parts = []
            for a in range(q_blocks):
                bias = bias_ref[0, jnp.maximum(i * q_blocks + a - (c * k_blocks + t), 0)]
                drop = mask_ref[0, a, c * k_blocks + t].astype(F32)
                parts.append(s[t * QB:(t + 1) * QB, a * QB:(a + 1) * QB] + bias + drop)
            rows.append(jnp.concatenate(parts, axis=1))
        s = jnp.concatenate(rows, axis=0)
        m_old = m_ref[...]
        m_new = jnp.maximum(m_old, jnp.max(s, axis=0, keepdims=True))
        alpha = jnp.exp2(m_old - m_new)
        p = jnp.exp2(s - m_new)
        m_ref[...] = m_new
        vt = [vt_ref[0, 0, c * (KTILE // QT) + j] for j in range(KTILE // QT)]
        v_ones = jnp.concatenate([jnp.concatenate(vt, axis=1), ones], axis=0)
        acc_ref[...] = alpha * acc_ref[...] + jnp.dot(v_ones, p.astype(BF16),
                                                      preferred_element_type=F32)
        return carry

    lax.fori_loop(0, (i + 1) * (QTILE // KTILE), body, 0)
    out = acc_ref[0:HEAD_DIM, :] / acc_ref[HEAD_DIM:HEAD_DIM + 1, :]
    o_ref[0] = out.T.astype(o_ref.dtype)


def _sparse_attention(qk, vt, toep_bias, mask):
    B, _, S, _ = qk.shape
    n_kb = S // QB
    return pl.pallas_call(
        _sparse_attn_kernel,
        grid=(B, N_HEADS_B, S // QTILE),
        in_specs=[pl.BlockSpec((1, 1, QTILE, HEAD_DIM), lambda b, h, i: (b, h, i, 0)),
                  pl.BlockSpec((1, 1, S, HEAD_DIM), lambda b, h, i: (b, N_HEADS_B + h, 0, 0)),
                  pl.BlockSpec((1, 1, S // QT, HEAD_DIM, QT), lambda b, h, i: (b, h, 0, 0, 0)),
                  pl.BlockSpec((1, n_kb, QB, QB), lambda b, h, i: (h, 0, 0, 0)),
                  pl.BlockSpec((1, QTILE // QB, n_kb, QB, QB), lambda b, h, i: (b, i, 0, 0, 0))],
        out_specs=pl.BlockSpec((1, QTILE, HEAD_DIM), lambda b, h, i: (b, i, h)),
        out_shape=jax.ShapeDtypeStruct((B, S, WIDTH_B), BF16),
        scratch_shapes=[pltpu.VMEM((1, QTILE), F32),
                        pltpu.VMEM((HEAD_DIM + ONES_ROWS, QTILE), F32)],
        compiler_params=_params(("parallel", "parallel", "parallel")),
        name="sparse_attn",
    )(qk, qk, vt, toep_bias, mask)


def _merge_kernel(ya_ref, yb_ref, ga_ref, gb_ref, wa_ref, wb_ref, out_ref):
    a = jnp.dot(ya_ref[...], wa_ref[...].astype(BF16), preferred_element_type=F32)
    b = jnp.dot(yb_ref[...], wb_ref[...].astype(BF16), preferred_element_type=F32)
    out_ref[...] = (ga_ref[...].astype(F32) * a + gb_ref[...].astype(F32) * b).astype(out_ref.dtype)


def _merge(ya, yb, gates, wa, wb, *, tm, tn):
    M = yb.shape[0]
    N = wa.cols
    return pl.pallas_call(
        _merge_kernel,
        grid=(M // tm, N // tn),
        in_specs=[pl.BlockSpec((tm, WIDTH_A_OUT), lambda i, j: (i, 0)),
                  pl.BlockSpec((tm, WIDTH_B), lambda i, j: (i, 0)),
                  pl.BlockSpec((tm, tn), lambda i, j: (i, j)),
                  pl.BlockSpec((tm, tn), lambda i, j: (i, j + N // tn)),
                  wa.spec(tn), wb.spec(tn)],
        out_specs=pl.BlockSpec((tm, tn), lambda i, j: (i, j)),
        out_shape=jax.ShapeDtypeStruct((M, N), BF16),
        compiler_params=_params(("parallel", "parallel")),
        name="gated_merge",
    )(ya, yb, gates, gates, wa.stacked, wb.stacked)


def _matmul_residual_kernel(a_ref, w_ref, r_ref, o_ref):
    o_ref[...] = r_ref[...] + jnp.dot(a_ref[...], w_ref[...].astype(BF16),
                                      preferred_element_type=F32)


def _matmul_residual(a, w, r, *, tm, tn, name):
    M, K = a.shape
    N = w.cols
    return pl.pallas_call(
        _matmul_residual_kernel,
        grid=(M // tm, N // tn),
        in_specs=[pl.BlockSpec((tm, K), lambda i, j: (i, 0)), w.spec(tn),
                  pl.BlockSpec((tm, tn), lambda i, j: (i, j))],
        out_specs=pl.BlockSpec((tm, tn), lambda i, j: (i, j)),
        out_shape=jax.ShapeDtypeStruct((M, N), F32),
        compiler_params=_params(("parallel", "parallel")),
        name=name,
    )(a, w.stacked, r)


def _residual_norm_kernel(a_ref, w_ref, r_ref, g_ref, x_ref, h_ref):
    x = r_ref[...] + jnp.dot(a_ref[...], w_ref[...].astype(BF16), preferred_element_type=F32)
    x_ref[...] = x
    y = x * lax.rsqrt(jnp.mean(x * x, axis=-1, keepdims=True) + RMS_EPS)
    h_ref[...] = (y * g_ref[...]).astype(h_ref.dtype)


def _matmul_residual_norm(a, w, r, g, *, tm, name):
    M, K = a.shape
    N = w.cols
    whole_rows = pl.BlockSpec((tm, N), lambda i, j: (i, 0))
    return pl.pallas_call(
        _residual_norm_kernel,
        grid=(M // tm, 1),
        in_specs=[pl.BlockSpec((tm, K), lambda i, j: (i, 0)), w.spec(N), whole_rows,
                  pl.BlockSpec((1, N), lambda i, j: (0, 0))],
        out_specs=[whole_rows, whole_rows],
        out_shape=[jax.ShapeDtypeStruct((M, N), F32), jax.ShapeDtypeStruct((M, N), BF16)],
        compiler_params=_params(("parallel", "arbitrary")),
        name=name,
    )(a, w.stacked, r, g.reshape(1, N))


def _swiglu_kernel(h_ref, wg_ref, wu_ref, o_ref):
    h = h_ref[...]
    gate = jnp.dot(h, wg_ref[...].astype(BF16), preferred_element_type=F32)
    up = jnp.dot(h, wu_ref[...].astype(BF16), preferred_element_type=F32)
    o_ref[...] = (gate * _sigmoid(gate) * up).astype(o_ref.dtype)


def _swiglu(h, wg, wu, *, tm, tn):
    M, D = h.shape
    N = wg.cols
    return pl.pallas_call(
        _swiglu_kernel,
        grid=(M // tm, N // tn),
        in_specs=[pl.BlockSpec((tm, D), lambda i, j: (i, 0)), wg.spec(tn), wu.spec(tn)],
        out_specs=pl.BlockSpec((tm, tn), lambda i, j: (i, j)),
        out_shape=jax.ShapeDtypeStruct((M, N), BF16),
        compiler_params=_params(("parallel", "parallel")),
        name="swiglu_gate_up",
    )(h, wg.stacked, wu.stacked)


def kernel(x, w_in, w_out_a, w_out_b, w_out, w_ff_gate, w_ff_up, w_ff_down,
           g_mix, g_ffn, g_final, rel_bias):
    B, S, D = x.shape
    depth = w_in.shape[0]
    M = B * S
    assert S % TS == 0 and S % QTILE == 0
    assert w_in.shape[2] == QKV_WIDTH + IDX_WIDTH + 2 * D
    topk = min(TOPK_MAX, S // 4)
    tm, tm_small, tn = 2048, 1024, 512

    band_bias, toep_bias = _bias_tiles(rel_bias, S // QB)
    x2 = x.reshape(M, D)
    h = _rmsnorm(x2, g_mix[0], tm=tm_small, out_dtype=BF16, name="rmsnorm_mix")
    for layer in range(depth):
        wl = w_in[layer]
        w_qkv = wl[:, :QKV_WIDTH].astype(BF16)[None]
        w_a = Weight(w_qkv, 0, 0, 3 * WIDTH_A)
        w_qkb = Weight(w_qkv, 0, 3 * WIDTH_A, 2 * WIDTH_B)
        w_vb = Weight(w_qkv, 0, QKV_WIDTH - WIDTH_B, WIDTH_B)
        w_idx = _whole(jnp.pad(wl[:, QKV_WIDTH:QKV_WIDTH + IDX_WIDTH],
                               ((0, 0), (0, IDX_PAD - IDX_WIDTH))).astype(BF16)[None], 0)
        w_gate = _whole(wl[:, QKV_WIDTH + IDX_WIDTH:].astype(BF16)[None], 0)

        qkv_a = _matmul(h, w_a, tm=tm, tn=tn, out_dtype=F32, layout="head",
                        batch_seq=(B, S), name="proj_a")
        qk_b = _matmul(h, w_qkb, tm=tm, tn=tn, out_dtype=BF16, layout="head",
                       batch_seq=(B, S), name="proj_qk_b")
        vt = _matmul(h, w_vb, tm=tm_small, tn=tn, out_dtype=BF16, layout="head_t",
                     batch_seq=(B, S), name="proj_vt")
        idx = _matmul(h, w_idx, tm=tm_small, tn=IDX_PAD, out_dtype=F32, name="proj_idx")
        gates = _matmul(h, w_gate, tm=tm, tn=tn, out_dtype=BF16, layout="sigmoid",
                        name="proj_gates")

        y_a = _dilated_attention(qkv_a, band_bias)
        mask = _indexer_mask(idx.reshape(B, S, IDX_PAD), topk)
        y_b = _sparse_attention(qk_b, vt, toep_bias, mask).reshape(M, WIDTH_B)

        merged = _merge(y_a, y_b, gates, _whole(w_out_a, layer), _whole(w_out_b, layer),
                        tm=tm, tn=tn)
        x2, h = _matmul_residual_norm(merged, _whole(w_out[layer].astype(BF16)[None], 0), x2,
                                      g_ffn[layer], tm=tm_small // 2, name="out_proj_norm")
        act = _swiglu(h, _whole(w_ff_gate, layer), _whole(w_ff_up, layer), tm=tm, tn=tn)
        x2 = _matmul_residual(act, _whole(w_ff_down[layer].astype(BF16)[None], 0), x2,
                              tm=tm_small, tn=tn, name="ffn_down")
        if layer + 1 < depth:
            h = _rmsnorm(x2, g_mix[layer + 1], tm=tm_small, out_dtype=BF16, name="rmsnorm_mix")
    return _rmsnorm(x2, g_final, tm=tm_small, out_dtype=F32, name="final_rmsnorm").reshape(B, S, D)
```

```python
import functools
import math
from typing import NamedTuple

import jax
import jax.numpy as jnp
from jax import lax
from jax.experimental import pallas as pl
from jax.experimental.pallas import tpu as pltpu

F32 = jnp.float32
BF16 = jnp.bfloat16

HEAD_DIM = 128
DILATIONS = (1, 4, 16)
BAND = 128
HEADS_PER_GROUP = 4
N_HEADS_A = 12
N_HEADS_B = 8
WIDTH_A = N_HEADS_A * HEAD_DIM
WIDTH_A_OUT = HEADS_PER_GROUP * HEAD_DIM
WIDTH_B = N_HEADS_B * HEAD_DIM
N_IDX_HEADS = 16
IDX_DIM = 64
TOPK_MAX = 256
REL_BUCKETS = 32
REL_MAX_DIST = 2048
RMS_EPS = 1e-6
QKV_WIDTH = 3 * WIDTH_A + 3 * WIDTH_B
IDX_WIDTH = N_IDX_HEADS * IDX_DIM + IDX_DIM + N_IDX_HEADS
IDX_PAD = 1152
KI_COL = N_IDX_HEADS * IDX_DIM
WI_COL = KI_COL + IDX_DIM
LANES = 128
QB = 128
KCHUNK = 4
QT = KCHUNK * QB
QTILE = 2 * QT
KTILE = 2 * QT
TS = BAND * max(DILATIONS)
NEG = -(2.0 ** 100)
INT_MIN = -(2 ** 31)
LOG2E = math.log2(math.e)
SPARSE_Q_SCALE = HEAD_DIM ** -0.5 * LOG2E
ONES_ROWS = 16
VMEM_LIMIT = 56 * 1024 * 1024


def _params(sem):
    return pltpu.CompilerParams(dimension_semantics=sem, vmem_limit_bytes=VMEM_LIMIT)


def _sigmoid(x):
    return 0.5 * jnp.tanh(0.5 * x) + 0.5


class Weight(NamedTuple):
    stacked: jax.Array
    layer: int
    col0: int
    cols: int

    def spec(self, tn):
        assert self.col0 % tn == 0 and self.cols % tn == 0
        first = self.col0 // tn
        return pl.BlockSpec((None, self.stacked.shape[1], tn),
                            lambda i, j, layer=self.layer: (layer, 0, first + j))


def _whole(stacked, layer):
    return Weight(stacked, layer, 0, stacked.shape[2])


def _rmsnorm_kernel(x_ref, g_ref, o_ref):
    x = x_ref[...]
    y = x * lax.rsqrt(jnp.mean(x * x, axis=-1, keepdims=True) + RMS_EPS)
    o_ref[...] = (y * g_ref[...]).astype(o_ref.dtype)


def _rmsnorm(x2, g, *, tm, out_dtype, name):
    M, D = x2.shape
    return pl.pallas_call(
        _rmsnorm_kernel,
        grid=(M // tm,),
        in_specs=[pl.BlockSpec((tm, D), lambda i: (i, 0)), pl.BlockSpec((1, D), lambda i: (0, 0))],
        out_specs=pl.BlockSpec((tm, D), lambda i: (i, 0)),
        out_shape=jax.ShapeDtypeStruct((M, D), out_dtype),
        compiler_params=_params(("parallel",)),
        name=name,
    )(x2, g.reshape(1, D))


def _matmul_kernel(a_ref, w_ref, o_ref, *, layout, lead_tiles, lead_scale):
    acc = jnp.dot(a_ref[...], w_ref[...].astype(BF16), preferred_element_type=F32)
    if lead_tiles:
        acc = acc * jnp.where(pl.program_id(1) < lead_tiles, lead_scale, 1.0)
    if layout == "rows":
        o_ref[...] = acc.astype(o_ref.dtype)
    elif layout == "sigmoid":
        o_ref[...] = _sigmoid(acc).astype(o_ref.dtype)
    else:
        for c in range(acc.shape[1] // HEAD_DIM):
            head = acc[:, c * HEAD_DIM:(c + 1) * HEAD_DIM]
            if layout == "head":
                o_ref[0, c] = head.astype(o_ref.dtype)
            else:
                for r in range(acc.shape[0] // QT):
                    o_ref[0, c, r] = head[r * QT:(r + 1) * QT].T.astype(o_ref.dtype)


def _matmul(a, w, *, tm, tn, out_dtype, layout="rows", batch_seq=None, lead_cols=0,
            lead_scale=1.0, name):
    assert lead_cols % tn == 0
    M, K = a.shape
    N = w.cols
    if layout in ("rows", "sigmoid"):
        out_shape = jax.ShapeDtypeStruct((M, N), out_dtype)
        out_spec = pl.BlockSpec((tm, tn), lambda i, j: (i, j))
    else:
        B, S = batch_seq
        per_b = S // tm
        if layout == "head":
            out_shape = jax.ShapeDtypeStruct((B, N // HEAD_DIM, S, HEAD_DIM), out_dtype)
            out_spec = pl.BlockSpec((1, tn // HEAD_DIM, tm, HEAD_DIM),
                                    lambda i, j: (i // per_b, j, i % per_b, 0))
        else:
            out_shape = jax.ShapeDtypeStruct((B, N // HEAD_DIM, S // QT, HEAD_DIM, QT), out_dtype)
            out_spec = pl.BlockSpec((1, tn // HEAD_DIM, tm // QT, HEAD_DIM, QT),
                                    lambda i, j: (i // per_b, j, i % per_b, 0, 0))
    return pl.pallas_call(
        functools.partial(_matmul_kernel, layout=layout, lead_tiles=lead_cols // tn,
                          lead_scale=lead_scale),
        grid=(M // tm, N // tn),
        in_specs=[pl.BlockSpec((tm, K), lambda i, j: (i, 0)), w.spec(tn)],
        out_specs=out_spec,
        out_shape=out_shape,
        compiler_params=_params(("parallel", "parallel")),
        name=name,
    )(a, w.stacked)


def _t5_bucket(dist):
    n = jnp.maximum(dist, 0)
    max_exact = REL_BUCKETS // 2
    nf = jnp.maximum(n, 1).astype(F32)
    large = max_exact + (jnp.log(nf / max_exact) / math.log(REL_MAX_DIST / max_exact)
                         * (REL_BUCKETS - max_exact)).astype(jnp.int32)
    large = jnp.minimum(large, REL_BUCKETS - 1)
    return jnp.where(n < max_exact, n, large)


def _table_lookup(tab_ref, bucket, col):
    val = jnp.zeros(bucket.shape, F32)
    for b in range(REL_BUCKETS):
        val = jnp.where(bucket == b, tab_ref[b, col], val)
    return val


def _band_bias_kernel(tab_ref, o_ref):
    h = pl.program_id(0)
    dilation = jnp.left_shift(1, 2 * (h // HEADS_PER_GROUP))
    qi = lax.broadcasted_iota(jnp.int32, (BAND, 2 * BAND), 0)
    kj = lax.broadcasted_iota(jnp.int32, (BAND, 2 * BAND), 1)
    delta = qi + BAND - kj
    bias = _table_lookup(tab_ref, _t5_bucket(delta * dilation), h)
    o_ref[0] = jnp.where((delta >= 0) & (delta <= BAND), bias, -jnp.inf)


def _toeplitz_bias_kernel(tab_ref, o_ref):
    h = pl.program_id(0)
    kj = lax.broadcasted_iota(jnp.int32, (QB, QB), 0)
    qi = lax.broadcasted_iota(jnp.int32, (QB, QB), 1)

    def body(d, carry):
        dist = d * QB + qi - kj
        o_ref[0, d] = _table_lookup(tab_ref, _t5_bucket(dist), N_HEADS_A + h) * LOG2E
        return carry

    lax.fori_loop(0, o_ref.shape[1], body, 0)


def _bias_tiles(rel_bias, n_blocks):
    smem = pl.BlockSpec(memory_space=pltpu.SMEM)
    band = pl.pallas_call(
        _band_bias_kernel,
        grid=(N_HEADS_A,),
        in_specs=[smem],
        out_specs=pl.BlockSpec((1, BAND, 2 * BAND), lambda h: (h, 0, 0)),
        out_shape=jax.ShapeDtypeStruct((N_HEADS_A, BAND, 2 * BAND), F32),
        name="band_bias",
    )(rel_bias)
    toep = pl.pallas_call(
        _toeplitz_bias_kernel,
        grid=(N_HEADS_B,),
        in_specs=[smem],
        out_specs=pl.BlockSpec((1, n_blocks, QB, QB), lambda h: (h, 0, 0, 0)),
        out_shape=jax.ShapeDtypeStruct((N_HEADS_B, n_blocks, QB, QB), F32),
        name="toeplitz_bias",
    )(rel_bias)
    return band, toep


def _dilated_kernel(*refs):
    n_g = len(DILATIONS)
    groups = [refs[6 * g:6 * g + 6] for g in range(n_g)]
    y_ref, o_scr, l_scr = refs[6 * n_g:]
    i = pl.program_id(2)
    scale = HEAD_DIM ** -0.5
    col = lax.broadcasted_iota(jnp.int32, (BAND, 2 * BAND), 1)

    for g, d in enumerate(DILATIONS):
        q_ref, k_ref, v_ref, kp_ref, vp_ref, bias_ref = groups[g]
        span = d * BAND
        bias = bias_ref[0]
        bias_first = jnp.where((i > 0) | (col >= BAND), bias, -jnp.inf)
        for r in range(d):
            prev = pl.ds(r, BAND, stride=d)
            k_prev, v_prev = kp_ref[0, 0, prev, :].astype(BF16), vp_ref[0, 0, prev, :].astype(BF16)
            for n in range(TS // span):
                rows = pl.ds(n * span + r, BAND, stride=d)
                b = bias_first if n == 0 else bias
                k_cur, v_cur = k_ref[0, 0, rows, :].astype(BF16), v_ref[0, 0, rows, :].astype(BF16)
                kwin = jnp.concatenate([k_prev, k_cur], axis=0)
                vwin = jnp.concatenate([v_prev, v_cur], axis=0)
                k_prev, v_prev = k_cur, v_cur
                q = q_ref[0, 0, rows, :].astype(BF16)
                s = lax.dot_general(q, kwin, (((1,), (1,)), ((), ())), preferred_element_type=F32)
                s = s * scale + b
                m = jnp.max(s, axis=-1, keepdims=True)
                p = jnp.exp(s - m)
                l = jnp.sum(p, axis=-1, keepdims=True)
                o_scr[g, rows, :] = jnp.dot(p.astype(BF16), vwin, preferred_element_type=F32) / l
                l_scr[g, rows, :] = jnp.broadcast_to(m + jnp.log(l), (BAND, HEAD_DIM))

    lses = [l_scr[g] for g in range(n_g)]
    mx = functools.reduce(jnp.maximum, lses)
    es = [jnp.exp(lse - mx) for lse in lses]
    tot = functools.reduce(jnp.add, es)
    y = functools.reduce(jnp.add, [(e / tot) * o_scr[g] for g, e in enumerate(es)])
    y_ref[...] = y.astype(y_ref.dtype)


def _dilated_attention(qkv_a, band_bias):
    B, _, S, _ = qkv_a.shape
    tiles = S // TS
    in_specs, args = [], []
    for g, d in enumerate(DILATIONS):
        span = d * BAND
        per_tile = TS // span
        for base in (0, N_HEADS_A, 2 * N_HEADS_A):
            in_specs.append(pl.BlockSpec(
                (1, 1, TS, HEAD_DIM),
                lambda b, h, i, hd=base + g * HEADS_PER_GROUP: (b, hd + h, i, 0)))
        for base in (N_HEADS_A, 2 * N_HEADS_A):
            in_specs.append(pl.BlockSpec(
                (1, 1, span, HEAD_DIM),
                lambda b, h, i, hd=base + g * HEADS_PER_GROUP, pt=per_tile:
                (b, hd + h, jnp.maximum(i * pt - 1, 0), 0)))
        in_specs.append(pl.BlockSpec((1, BAND, 2 * BAND),
                                     lambda b, h, i, g=g: (g * HEADS_PER_GROUP + h, 0, 0)))
        args += [qkv_a] * 5 + [band_bias]
    return pl.pallas_call(
        _dilated_kernel,
        grid=(B, HEADS_PER_GROUP, tiles),
        in_specs=in_specs,
        out_specs=pl.BlockSpec((TS, HEAD_DIM), lambda b, h, i: (b * tiles + i, h)),
        out_shape=jax.ShapeDtypeStruct((B * S, WIDTH_A_OUT), BF16),
        scratch_shapes=[pltpu.VMEM((len(DILATIONS), TS, HEAD_DIM), F32),
                        pltpu.VMEM((len(DILATIONS), TS, HEAD_DIM), F32)],
        compiler_params=_params(("parallel", "parallel", "parallel")),
        name="dilated_attn",
    )(*args)


def _indexer_kernel(q_ref, k_ref, mask_ref, qh_ref, wb_ref, key_ref, cut_ref, *, n_kb, topk):
    i = pl.program_id(1)
    n_vis = i + 1
    n_chunks = (i + KCHUNK) // KCHUNK
    sub = QB // 8
    for h in range(N_IDX_HEADS):
        qh_ref[h * QB:(h + 1) * QB, :] = q_ref[0, :, h * IDX_DIM:(h + 1) * IDX_DIM].astype(BF16)
    wt = q_ref[0, :, KI_COL:KI_COL + LANES].T * ((N_IDX_HEADS ** -0.5) * (IDX_DIM ** -0.5))
    for h in range(N_IDX_HEADS):
        wb_ref[h] = jnp.broadcast_to(wt[IDX_DIM + h:IDX_DIM + h + 1, :], (8, QB))

    key_pos = (lax.broadcasted_iota(jnp.int32, (sub, 8, QB), 0) * 8
               + lax.broadcasted_iota(jnp.int32, (sub, 8, QB), 1))
    diag_vis = key_pos <= lax.broadcasted_iota(jnp.int32, (sub, 8, QB), 2)

    def score_body(c, carry):
        row0 = pl.multiple_of(c * QT, QT)
        kchunk = k_ref[0, pl.ds(row0, QT), 0:IDX_DIM].astype(BF16)
        accs = [jnp.zeros((sub, 8, QB), F32) for _ in range(KCHUNK)]
        for pair in range(N_IDX_HEADS // 2):
            d = lax.dot_general(kchunk, qh_ref[pair * 2 * QB:(pair + 1) * 2 * QB, :],
                                (((1,), (1,)), ((), ())), preferred_element_type=F32)
            for u in range(2):
                w = wb_ref[2 * pair + u][None]
                for t in range(KCHUNK):
                    tile = d[t * QB:(t + 1) * QB, u * QB:(u + 1) * QB].reshape(sub, 8, QB)
                    accs[t] = accs[t] + jnp.maximum(tile, 0.0) * w
        for t in range(KCHUNK):
            kb = c * KCHUNK + t
            vis = (kb < i) | ((kb == i) & diag_vis)
            bits = pltpu.bitcast(jnp.where(vis, accs[t], -jnp.inf), jnp.int32)
            key_ref[pl.ds(pl.multiple_of(kb * sub, sub), sub)] = bits ^ ((bits >> 31) & 0x7FFFFFFF)
        return carry

    lax.fori_loop(0, n_chunks, score_body, 0)

    def count(pred):
        def count_body(c, cnt):
            for u in range(KCHUNK):
                kb = c * KCHUNK + u
                keys = key_ref[pl.ds(pl.multiple_of(kb * sub, sub), sub)]
                cnt = cnt + jnp.sum(pred(keys, kb).astype(jnp.int32), axis=0)
            return cnt

        cnt = lax.fori_loop(0, n_chunks, count_body, jnp.zeros((8, QB), jnp.int32))
        return jnp.sum(cnt, axis=0, keepdims=True)

    def rows(v):
        return jnp.broadcast_to(v, (8, QB))[None]

    def bit_body(it, state):
        t, cnt_t = state
        cand = t + jnp.left_shift(1, 31 - it)
        cand_b = rows(cand)
        total = count(lambda keys, kb: keys >= cand_b)
        take = total >= topk
        return jnp.where(take, cand, t), jnp.where(take, total, cnt_t)

    n_keys = n_kb * QB
    thr, cnt_thr = lax.fori_loop(
        0, 32, bit_body,
        (jnp.full((1, QB), INT_MIN, jnp.int32), jnp.full((1, QB), n_keys + 1, jnp.int32)))
    thr_b = rows(thr)

    cut_ref[...] = jnp.full((1, QB), n_keys, jnp.int32)

    @pl.when(jnp.max(cnt_thr) > topk)
    def _():
        need = topk - count(lambda keys, kb: keys > thr_b)

        def pos_body(j, u):
            cand = u + jnp.left_shift(1, n_keys.bit_length() - 2 - j)
            cand_b = rows(cand)
            below = count(lambda keys, kb: (keys == thr_b) & (kb * QB + key_pos < cand_b))
            return jnp.where(below < need, cand, u)

        cut_ref[...] = lax.fori_loop(0, n_keys.bit_length() - 1, pos_body,
                                     jnp.zeros((1, QB), jnp.int32))

    cut_b = rows(cut_ref[...])

    def mask_body(kb, carry):
        keys = key_ref[pl.ds(pl.multiple_of(kb * sub, sub), sub)]
        sel = (keys > thr_b) | ((keys == thr_b) & (kb * QB + key_pos <= cut_b))
        sel = sel & ((kb < i) | diag_vis)
        mask_ref[0, 0, kb] = jnp.where(sel, 0.0, NEG).reshape(QB, QB).astype(mask_ref.dtype)
        return carry

    lax.fori_loop(0, n_vis, mask_body, 0)

    def zero_body(kb, carry):
        mask_ref[0, 0, kb] = jnp.full((QB, QB), NEG, mask_ref.dtype)
        return carry

    lax.fori_loop(n_vis, n_kb, zero_body, 0)


def _indexer_mask(idx, topk):
    B, S, _ = idx.shape
    n_kb = S // QB
    return pl.pallas_call(
        functools.partial(_indexer_kernel, n_kb=n_kb, topk=topk),
        grid=(B, n_kb),
        in_specs=[pl.BlockSpec((1, QB, IDX_PAD), lambda b, i: (b, i, 0)),
                  pl.BlockSpec((1, S, LANES), lambda b, i: (b, 0, KI_COL // LANES))],
        out_specs=pl.BlockSpec((1, 1, n_kb, QB, QB), lambda b, i: (b, i, 0, 0, 0)),
        out_shape=jax.ShapeDtypeStruct((B, n_kb, n_kb, QB, QB), BF16),
        scratch_shapes=[pltpu.VMEM((N_IDX_HEADS * QB, IDX_DIM), BF16),
                        pltpu.VMEM((N_IDX_HEADS, 8, QB), F32),
                        pltpu.VMEM((n_kb * QB // 8, 8, QB), jnp.int32),
                        pltpu.VMEM((1, QB), jnp.int32)],
        compiler_params=_params(("parallel", "parallel")),
        name="indexer_topk_mask",
    )(idx, idx)


def _sparse_attn_kernel(q_ref, k_ref, vt_ref, bias_ref, mask_ref, o_ref, m_ref, acc_ref):
    i = pl.program_id(2)
    q = q_ref[0, 0]
    m_ref[...] = jnp.full(m_ref.shape, NEG, F32)
    acc_ref[...] = jnp.zeros(acc_ref.shape, F32)
    ones = jnp.ones((ONES_ROWS, KTILE), BF16)
    q_blocks = QTILE // QB
    k_blocks = KTILE // QB

    def body(c, carry):
        row0 = pl.multiple_of(c * KTILE, KTILE)
        s = lax.dot_general(k_ref[0, 0, pl.ds(row0, KTILE), :], q, (((1,), (1,)), ((), ())),
                            preferred_element_type=F32)
        rows = []
        for t in range(k_blocks):
            parts = []
            for a in range(q_blocks):
                bias = bias_ref[0, jnp.maximum(i * q_blocks + a - (c * k_blocks + t), 0)]
                drop = mask_ref[0, a, c * k_blocks + t].astype(F32)
                parts.append(s[t * QB:(t + 1) * QB, a * QB:(a + 1) * QB] + bias + drop)
            rows.append(jnp.concatenate(parts, axis=1))
        s = jnp.concatenate(rows, axis=0)
        m_old = m_ref[...]
        m_new = jnp.maximum(m_old, jnp.max(s, axis=0, keepdims=True))
        alpha = jnp.exp2(m_old - m_new)
        p = jnp.exp2(s - m_new)
        m_ref[...] = m_new
        vt = [vt_ref[0, 0, c * (KTILE // QT) + j] for j in range(KTILE // QT)]
        v_ones = jnp.concatenate([jnp.concatenate(vt, axis=1), ones], axis=0)
        acc_ref[...] = alpha * acc_ref[...] + jnp.dot(v_ones, p.astype(BF16),
                                                      preferred_element_type=F32)
        return carry

    lax.fori_loop(0, (i + 1) * (QTILE // KTILE), body, 0)
    out = acc_ref[0:HEAD_DIM, :] / acc_ref[HEAD_DIM:HEAD_DIM + 1, :]
    o_ref[0] = out.T.astype(o_ref.dtype)


def _sparse_attention(qk, vt, toep_bias, mask):
    B, _, S, _ = qk.shape
    n_kb = S // QB
    return pl.pallas_call(
        _sparse_attn_kernel,
        grid=(B, N_HEADS_B, S // QTILE),
        in_specs=[pl.BlockSpec((1, 1, QTILE, HEAD_DIM), lambda b, h, i: (b, h, i, 0)),
                  pl.BlockSpec((1, 1, S, HEAD_DIM), lambda b, h, i: (b, N_HEADS_B + h, 0, 0)),
                  pl.BlockSpec((1, 1, S // QT, HEAD_DIM, QT), lambda b, h, i: (b, h, 0, 0, 0)),
                  pl.BlockSpec((1, n_kb, QB, QB), lambda b, h, i: (h, 0, 0, 0)),
                  pl.BlockSpec((1, QTILE // QB, n_kb, QB, QB), lambda b, h, i: (b, i, 0, 0, 0))],
        out_specs=pl.BlockSpec((1, QTILE, HEAD_DIM), lambda b, h, i: (b, i, h)),
        out_shape=jax.ShapeDtypeStruct((B, S, WIDTH_B), BF16),
        scratch_shapes=[pltpu.VMEM((1, QTILE), F32),
                        pltpu.VMEM((HEAD_DIM + ONES_ROWS, QTILE), F32)],
        compiler_params=_params(("parallel", "parallel", "parallel")),
        name="sparse_attn",
    )(qk, qk, vt, toep_bias, mask)


def _merge_kernel(ya_ref, yb_ref, ga_ref, gb_ref, wa_ref, wb_ref, out_ref):
    a = jnp.dot(ya_ref[...], wa_ref[...].astype(BF16), preferred_element_type=F32)
    b = jnp.dot(yb_ref[...], wb_ref[...].astype(BF16), preferred_element_type=F32)
    out_ref[...] = (ga_ref[...].astype(F32) * a + gb_ref[...].astype(F32) * b).astype(out_ref.dtype)


def _merge(ya, yb, gates, wa, wb, *, tm, tn):
    M = yb.shape[0]
    N = wa.cols
    return pl.pallas_call(
        _merge_kernel,
        grid=(M // tm, N // tn),
        in_specs=[pl.BlockSpec((tm, WIDTH_A_OUT), lambda i, j: (i, 0)),
                  pl.BlockSpec((tm, WIDTH_B), lambda i, j: (i, 0)),
                  pl.BlockSpec((tm, tn), lambda i, j: (i, j)),
                  pl.BlockSpec((tm, tn), lambda i, j: (i, j + N // tn)),
                  wa.spec(tn), wb.spec(tn)],
        out_specs=pl.BlockSpec((tm, tn), lambda i, j: (i, j)),
        out_shape=jax.ShapeDtypeStruct((M, N), BF16),
        compiler_params=_params(("parallel", "parallel")),
        name="gated_merge",
    )(ya, yb, gates, gates, wa.stacked, wb.stacked)


def _matmul_residual_kernel(a_ref, w_ref, r_ref, o_ref):
    o_ref[...] = r_ref[...] + jnp.dot(a_ref[...], w_ref[...].astype(BF16),
                                      preferred_element_type=F32)


def _matmul_residual(a, w, r, *, tm, tn, name):
    M, K = a.shape
    N = w.cols
    return pl.pallas_call(
        _matmul_residual_kernel,
        grid=(M // tm, N // tn),
        in_specs=[pl.BlockSpec((tm, K), lambda i, j: (i, 0)), w.spec(tn),
                  pl.BlockSpec((tm, tn), lambda i, j: (i, j))],
        out_specs=pl.BlockSpec((tm, tn), lambda i, j: (i, j)),
        out_shape=jax.ShapeDtypeStruct((M, N), F32),
        compiler_params=_params(("parallel", "parallel")),
        name=name,
    )(a, w.stacked, r)


def _residual_norm_kernel(a_ref, w_ref, r_ref, g_ref, x_ref, h_ref):
    x = r_ref[...] + jnp.dot(a_ref[...], w_ref[...].astype(BF16), preferred_element_type=F32)
    x_ref[...] = x
    y = x * lax.rsqrt(jnp.mean(x * x, axis=-1, keepdims=True) + RMS_EPS)
    h_ref[...] = (y * g_ref[...]).astype(h_ref.dtype)


def _matmul_residual_norm(a, w, r, g, *, tm, name):
    M, K = a.shape
    N = w.cols
    whole_rows = pl.BlockSpec((tm, N), lambda i, j: (i, 0))
    return pl.pallas_call(
        _residual_norm_kernel,
        grid=(M // tm, 1),
        in_specs=[pl.BlockSpec((tm, K), lambda i, j: (i, 0)), w.spec(N), whole_rows,
                  pl.BlockSpec((1, N), lambda i, j: (0, 0))],
        out_specs=[whole_rows, whole_rows],
        out_shape=[jax.ShapeDtypeStruct((M, N), F32), jax.ShapeDtypeStruct((M, N), BF16)],
        compiler_params=_params(("parallel", "arbitrary")),
        name=name,
    )(a, w.stacked, r, g.reshape(1, N))


def _swiglu_kernel(h_ref, wg_ref, wu_ref, o_ref):
    h = h_ref[...]
    gate = jnp.dot(h, wg_ref[...].astype(BF16), preferred_element_type=F32)
    up = jnp.dot(h, wu_ref[...].astype(BF16), preferred_element_type=F32)
    o_ref[...] = (gate * _sigmoid(gate) * up).astype(o_ref.dtype)


def _swiglu(h, wg, wu, *, tm, tn):
    M, D = h.shape
    N = wg.cols
    return pl.pallas_call(
        _swiglu_kernel,
        grid=(M // tm, N // tn),
        in_specs=[pl.BlockSpec((tm, D), lambda i, j: (i, 0)), wg.spec(tn), wu.spec(tn)],
        out_specs=pl.BlockSpec((tm, tn), lambda i, j: (i, j)),
        out_shape=jax.ShapeDtypeStruct((M, N), BF16),
        compiler_params=_params(("parallel", "parallel")),
        name="swiglu_gate_up",
    )(h, wg.stacked, wu.stacked)


def kernel(x, w_in, w_out_a, w_out_b, w_out, w_ff_gate, w_ff_up, w_ff_down,
           g_mix, g_ffn, g_final, rel_bias):
    B, S, D = x.shape
    depth = w_in.shape[0]
    M = B * S
    assert S % TS == 0 and S % QTILE == 0
    assert w_in.shape[2] == QKV_WIDTH + IDX_WIDTH + 2 * D
    topk = min(TOPK_MAX, S // 4)
    tm, tm_small, tn = 2048, 1024, 512

    band_bias, toep_bias = _bias_tiles(rel_bias, S // QB)
    x2 = x.reshape(M, D)
    h = _rmsnorm(x2, g_mix[0], tm=tm_small, out_dtype=BF16, name="rmsnorm_mix")
    for layer in range(depth):
        wl = w_in[layer]
        w_qkv = wl[:, :QKV_WIDTH].astype(BF16)[None]
        w_a = Weight(w_qkv, 0, 0, 3 * WIDTH_A)
        w_qkb = Weight(w_qkv, 0, 3 * WIDTH_A, 2 * WIDTH_B)
        w_vb = Weight(w_qkv, 0, QKV_WIDTH - WIDTH_B, WIDTH_B)
        w_idx = _whole(jnp.pad(wl[:, QKV_WIDTH:QKV_WIDTH + IDX_WIDTH],
                               ((0, 0), (0, IDX_PAD - IDX_WIDTH))).astype(BF16)[None], 0)
        w_gate = _whole(wl[:, QKV_WIDTH + IDX_WIDTH:].astype(BF16)[None], 0)

        qkv_a = _matmul(h, w_a, tm=tm, tn=tn, out_dtype=F32, layout="head",
                        batch_seq=(B, S), name="proj_a")
        qk_b = _matmul(h, w_qkb, tm=tm, tn=tn, out_dtype=BF16, layout="head", batch_seq=(B, S),
                       lead_cols=WIDTH_B, lead_scale=SPARSE_Q_SCALE, name="proj_qk_b")
        vt = _matmul(h, w_vb, tm=tm_small, tn=tn, out_dtype=BF16, layout="head_t",
                     batch_seq=(B, S), name="proj_vt")
        idx = _matmul(h, w_idx, tm=tm_small, tn=IDX_PAD, out_dtype=F32, name="proj_idx")
        gates = _matmul(h, w_gate, tm=tm, tn=tn, out_dtype=BF16, layout="sigmoid",
                        name="proj_gates")

        y_a = _dilated_attention(qkv_a, band_bias)
        mask = _indexer_mask(idx.reshape(B, S, IDX_PAD), topk)
        y_b = _sparse_attention(qk_b, vt, toep_bias, mask).reshape(M, WIDTH_B)

        merged = _merge(y_a, y_b, gates, _whole(w_out_a, layer), _whole(w_out_b, layer),
                        tm=tm, tn=tn)
        x2, h = _matmul_residual_norm(merged, _whole(w_out[layer].astype(BF16)[None], 0), x2,
                                      g_ffn[layer], tm=tm_small // 2, name="out_proj_norm")
        act = _swiglu(h, _whole(w_ff_gate, layer), _whole(w_ff_up, layer), tm=tm, tn=tn)
        x2 = _matmul_residual(act, _whole(w_ff_down[layer].astype(BF16)[None], 0), x2,
                              tm=tm_small, tn=tn, name="ffn_down")
        if layer + 1 < depth:
            h = _rmsnorm(x2, g_mix[layer + 1], tm=tm_small, out_dtype=BF16, name="rmsnorm_mix")
    return _rmsnorm(x2, g_final, tm=tm_small, out_dtype=F32, name="final_rmsnorm").reshape(B, S, D)
```

```python
import functools
import math
from typing import NamedTuple

import jax
import jax.numpy as jnp
from jax import lax
from jax.experimental import pallas as pl
from jax.experimental.pallas import tpu as pltpu

F32 = jnp.float32
BF16 = jnp.bfloat16

HEAD_DIM = 128
DILATIONS = (1, 4, 16)
BAND = 128
HEADS_PER_GROUP = 4
N_HEADS_A = 12
N_HEADS_B = 8
WIDTH_A = N_HEADS_A * HEAD_DIM
WIDTH_A_OUT = HEADS_PER_GROUP * HEAD_DIM
WIDTH_B = N_HEADS_B * HEAD_DIM
N_IDX_HEADS = 16
IDX_DIM = 64
TOPK_MAX = 256
REL_BUCKETS = 32
REL_MAX_DIST = 2048
RMS_EPS = 1e-6
QKV_WIDTH = 3 * WIDTH_A + 3 * WIDTH_B
IDX_WIDTH = N_IDX_HEADS * IDX_DIM + IDX_DIM + N_IDX_HEADS
IDX_PAD = 1152
KI_COL = N_IDX_HEADS * IDX_DIM
WI_COL = KI_COL + IDX_DIM
LANES = 128
QB = 128
KCHUNK = 4
QT = KCHUNK * QB
QTILE = 2 * QT
KTILE = 2 * QT
TS = BAND * max(DILATIONS)
NEG = -(2.0 ** 100)
INT_MIN = -(2 ** 31)
LOG2E = math.log2(math.e)
SPARSE_Q_SCALE = HEAD_DIM ** -0.5 * LOG2E
ONES_ROWS = 16
VMEM_LIMIT = 56 * 1024 * 1024


def _params(sem):
    return pltpu.CompilerParams(dimension_semantics=sem, vmem_limit_bytes=VMEM_LIMIT)


def _sigmoid(x):
    return 0.5 * jnp.tanh(0.5 * x) + 0.5


class Weight(NamedTuple):
    stacked: jax.Array
    layer: int
    col0: int
    cols: int

    def spec(self, tn):
        assert self.col0 % tn == 0 and self.cols % tn == 0
        first = self.col0 // tn
        return pl.BlockSpec((None, self.stacked.shape[1], tn),
                            lambda i, j, layer=self.layer: (layer, 0, first + j))


def _whole(stacked, layer):
    return Weight(stacked, layer, 0, stacked.shape[2])


def _rmsnorm_kernel(x_ref, g_ref, o_ref):
    x = x_ref[...]
    y = x * lax.rsqrt(jnp.mean(x * x, axis=-1, keepdims=True) + RMS_EPS)
    o_ref[...] = (y * g_ref[...]).astype(o_ref.dtype)


def _rmsnorm(x2, g, *, tm, out_dtype, name):
    M, D = x2.shape
    return pl.pallas_call(
        _rmsnorm_kernel,
        grid=(M // tm,),
        in_specs=[pl.BlockSpec((tm, D), lambda i: (i, 0)), pl.BlockSpec((1, D), lambda i: (0, 0))],
        out_specs=pl.BlockSpec((tm, D), lambda i: (i, 0)),
        out_shape=jax.ShapeDtypeStruct((M, D), out_dtype),
        compiler_params=_params(("parallel",)),
        name=name,
    )(x2, g.reshape(1, D))


def _matmul_kernel(a_ref, w_ref, o_ref, *, layout, lead_tiles, lead_scale):
    acc = jnp.dot(a_ref[...], w_ref[...].astype(BF16), preferred_element_type=F32)
    if lead_tiles:
        acc = acc * jnp.where(pl.program_id(1) < lead_tiles, lead_scale, 1.0)
    if layout == "rows":
        o_ref[...] = acc.astype(o_ref.dtype)
    elif layout == "sigmoid":
        o_ref[...] = _sigmoid(acc).astype(o_ref.dtype)
    else:
        for c in range(acc.shape[1] // HEAD_DIM):
            head = acc[:, c * HEAD_DIM:(c + 1) * HEAD_DIM]
            if layout == "head":
                o_ref[0, c] = head.astype(o_ref.dtype)
            else:
                for r in range(acc.shape[0] // QT):
                    o_ref[0, c, r] = head[r * QT:(r + 1) * QT].T.astype(o_ref.dtype)


def _matmul(a, w, *, tm, tn, out_dtype, layout="rows", batch_seq=None, lead_cols=0,
            lead_scale=1.0, name):
    assert lead_cols % tn == 0
    M, K = a.shape
    N = w.cols
    if layout in ("rows", "sigmoid"):
        out_shape = jax.ShapeDtypeStruct((M, N), out_dtype)
        out_spec = pl.BlockSpec((tm, tn), lambda i, j: (i, j))
    else:
        B, S = batch_seq
        per_b = S // tm
        if layout == "head":
            out_shape = jax.ShapeDtypeStruct((B, N // HEAD_DIM, S, HEAD_DIM), out_dtype)
            out_spec = pl.BlockSpec((1, tn // HEAD_DIM, tm, HEAD_DIM),
                                    lambda i, j: (i // per_b, j, i % per_b, 0))
        else:
            out_shape = jax.ShapeDtypeStruct((B, N // HEAD_DIM, S // QT, HEAD_DIM, QT), out_dtype)
            out_spec = pl.BlockSpec((1, tn // HEAD_DIM, tm // QT, HEAD_DIM, QT),
                                    lambda i, j: (i // per_b, j, i % per_b, 0, 0))
    return pl.pallas_call(
        functools.partial(_matmul_kernel, layout=layout, lead_tiles=lead_cols // tn,
                          lead_scale=lead_scale),
        grid=(M // tm, N // tn),
        in_specs=[pl.BlockSpec((tm, K), lambda i, j: (i, 0)), w.spec(tn)],
        out_specs=out_spec,
        out_shape=out_shape,
        compiler_params=_params(("parallel", "parallel")),
        name=name,
    )(a, w.stacked)


def _t5_bucket(dist):
    n = jnp.maximum(dist, 0)
    max_exact = REL_BUCKETS // 2
    nf = jnp.maximum(n, 1).astype(F32)
    large = max_exact + (jnp.log(nf / max_exact) / math.log(REL_MAX_DIST / max_exact)
                         * (REL_BUCKETS - max_exact)).astype(jnp.int32)
    large = jnp.minimum(large, REL_BUCKETS - 1)
    return jnp.where(n < max_exact, n, large)


def _table_lookup(tab_ref, bucket, col):
    val = jnp.zeros(bucket.shape, F32)
    for b in range(REL_BUCKETS):
        val = jnp.where(bucket == b, tab_ref[b, col], val)
    return val


def _band_bias_kernel(tab_ref, o_ref):
    h = pl.program_id(0)
    dilation = jnp.left_shift(1, 2 * (h // HEADS_PER_GROUP))
    qi = lax.broadcasted_iota(jnp.int32, (BAND, 2 * BAND), 0)
    kj = lax.broadcasted_iota(jnp.int32, (BAND, 2 * BAND), 1)
    delta = qi + BAND - kj
    bias = _table_lookup(tab_ref, _t5_bucket(delta * dilation), h)
    o_ref[0] = jnp.where((delta >= 0) & (delta <= BAND), bias, -jnp.inf)


def _toeplitz_bias_kernel(tab_ref, o_ref):
    h = pl.program_id(0)
    kj = lax.broadcasted_iota(jnp.int32, (QB, QB), 0)
    qi = lax.broadcasted_iota(jnp.int32, (QB, QB), 1)

    def body(d, carry):
        dist = d * QB + qi - kj
        o_ref[0, d] = _table_lookup(tab_ref, _t5_bucket(dist), N_HEADS_A + h) * LOG2E
        return carry

    lax.fori_loop(0, o_ref.shape[1], body, 0)


def _bias_tiles(rel_bias, n_blocks):
    smem = pl.BlockSpec(memory_space=pltpu.SMEM)
    band = pl.pallas_call(
        _band_bias_kernel,
        grid=(N_HEADS_A,),
        in_specs=[smem],
        out_specs=pl.BlockSpec((1, BAND, 2 * BAND), lambda h: (h, 0, 0)),
        out_shape=jax.ShapeDtypeStruct((N_HEADS_A, BAND, 2 * BAND), F32),
        name="band_bias",
    )(rel_bias)
    toep = pl.pallas_call(
        _toeplitz_bias_kernel,
        grid=(N_HEADS_B,),
        in_specs=[smem],
        out_specs=pl.BlockSpec((1, n_blocks, QB, QB), lambda h: (h, 0, 0, 0)),
        out_shape=jax.ShapeDtypeStruct((N_HEADS_B, n_blocks, QB, QB), F32),
        name="toeplitz_bias",
    )(rel_bias)
    return band, toep


def _dilated_kernel(*refs):
    n_g = len(DILATIONS)
    groups = [refs[6 * g:6 * g + 6] for g in range(n_g)]
    y_ref, o_scr, l_scr = refs[6 * n_g:]
    i = pl.program_id(2)
    scale = HEAD_DIM ** -0.5
    col = lax.broadcasted_iota(jnp.int32, (BAND, 2 * BAND), 1)

    for g, d in enumerate(DILATIONS):
        q_ref, k_ref, v_ref, kp_ref, vp_ref, bias_ref = groups[g]
        span = d * BAND
        bias = bias_ref[0]
        bias_first = jnp.where((i > 0) | (col >= BAND), bias, -jnp.inf)
        for r in range(d):
            prev = pl.ds(r, BAND, stride=d)
            k_prev, v_prev = kp_ref[0, 0, prev, :].astype(BF16), vp_ref[0, 0, prev, :].astype(BF16)
            for n in range(TS // span):
                rows = pl.ds(n * span + r, BAND, stride=d)
                b = bias_first if n == 0 else bias
                k_cur, v_cur = k_ref[0, 0, rows, :].astype(BF16), v_ref[0, 0, rows, :].astype(BF16)
                kwin = jnp.concatenate([k_prev, k_cur], axis=0)
                vwin = jnp.concatenate([v_prev, v_cur], axis=0)
                k_prev, v_prev = k_cur, v_cur
                q = q_ref[0, 0, rows, :].astype(BF16)
                s = lax.dot_general(q, kwin, (((1,), (1,)), ((), ())), preferred_element_type=F32)
                s = s * scale + b
                m = jnp.max(s, axis=-1, keepdims=True)
                p = jnp.exp(s - m)
                l = jnp.sum(p, axis=-1, keepdims=True)
                o_scr[g, rows, :] = jnp.dot(p.astype(BF16), vwin, preferred_element_type=F32) / l
                l_scr[g, rows, :] = jnp.broadcast_to(m + jnp.log(l), (BAND, HEAD_DIM))

    lses = [l_scr[g] for g in range(n_g)]
    mx = functools.reduce(jnp.maximum, lses)
    es = [jnp.exp(lse - mx) for lse in lses]
    tot = functools.reduce(jnp.add, es)
    y = functools.reduce(jnp.add, [(e / tot) * o_scr[g] for g, e in enumerate(es)])
    y_ref[...] = y.astype(y_ref.dtype)


def _dilated_attention(qkv_a, band_bias):
    B, _, S, _ = qkv_a.shape
    tiles = S // TS
    in_specs, args = [], []
    for g, d in enumerate(DILATIONS):
        span = d * BAND
        per_tile = TS // span
        for base in (0, N_HEADS_A, 2 * N_HEADS_A):
            in_specs.append(pl.BlockSpec(
                (1, 1, TS, HEAD_DIM),
                lambda b, h, i, hd=base + g * HEADS_PER_GROUP: (b, hd + h, i, 0)))
        for base in (N_HEADS_A, 2 * N_HEADS_A):
            in_specs.append(pl.BlockSpec(
                (1, 1, span, HEAD_DIM),
                lambda b, h, i, hd=base + g * HEADS_PER_GROUP, pt=per_tile:
                (b, hd + h, jnp.maximum(i * pt - 1, 0), 0)))
        in_specs.append(pl.BlockSpec((1, BAND, 2 * BAND),
                                     lambda b, h, i, g=g: (g * HEADS_PER_GROUP + h, 0, 0)))
        args += [qkv_a] * 5 + [band_bias]
    return pl.pallas_call(
        _dilated_kernel,
        grid=(B, HEADS_PER_GROUP, tiles),
        in_specs=in_specs,
        out_specs=pl.BlockSpec((TS, HEAD_DIM), lambda b, h, i: (b * tiles + i, h)),
        out_shape=jax.ShapeDtypeStruct((B * S, WIDTH_A_OUT), BF16),
        scratch_shapes=[pltpu.VMEM((len(DILATIONS), TS, HEAD_DIM), F32),
                        pltpu.VMEM((len(DILATIONS), TS, HEAD_DIM), F32)],
        compiler_params=_params(("parallel", "parallel", "parallel")),
        name="dilated_attn",
    )(*args)


def _indexer_kernel(q_ref, k_ref, mask_ref, qh_ref, wb_ref, key_ref, cut_ref, thr_ref, cnt_ref,
                    *, n_kb, topk):
    i = pl.program_id(1)
    n_vis = i + 1
    n_chunks = (i + KCHUNK) // KCHUNK
    sub = QB // 8
    for h in range(N_IDX_HEADS):
        qh_ref[h * QB:(h + 1) * QB, :] = q_ref[0, :, h * IDX_DIM:(h + 1) * IDX_DIM].astype(BF16)
    wt = q_ref[0, :, KI_COL:KI_COL + LANES].T * ((N_IDX_HEADS ** -0.5) * (IDX_DIM ** -0.5))
    for h in range(N_IDX_HEADS):
        wb_ref[h] = jnp.broadcast_to(wt[IDX_DIM + h:IDX_DIM + h + 1, :], (8, QB))

    key_pos = (lax.broadcasted_iota(jnp.int32, (sub, 8, QB), 0) * 8
               + lax.broadcasted_iota(jnp.int32, (sub, 8, QB), 1))
    diag_vis = key_pos <= lax.broadcasted_iota(jnp.int32, (sub, 8, QB), 2)

    def score_body(c, carry):
        row0 = pl.multiple_of(c * QT, QT)
        kchunk = k_ref[0, pl.ds(row0, QT), 0:IDX_DIM].astype(BF16)
        accs = [jnp.zeros((sub, 8, QB), F32) for _ in range(KCHUNK)]
        for pair in range(N_IDX_HEADS // 2):
            d = lax.dot_general(kchunk, qh_ref[pair * 2 * QB:(pair + 1) * 2 * QB, :],
                                (((1,), (1,)), ((), ())), preferred_element_type=F32)
            for u in range(2):
                w = wb_ref[2 * pair + u][None]
                for t in range(KCHUNK):
                    tile = d[t * QB:(t + 1) * QB, u * QB:(u + 1) * QB].reshape(sub, 8, QB)
                    accs[t] = accs[t] + jnp.maximum(tile, 0.0) * w
        for t in range(KCHUNK):
            kb = c * KCHUNK + t
            vis = (kb < i) | ((kb == i) & diag_vis)
            bits = pltpu.bitcast(jnp.where(vis, accs[t], -jnp.inf), jnp.int32)
            key_ref[pl.ds(pl.multiple_of(kb * sub, sub), sub)] = bits ^ ((bits >> 31) & 0x7FFFFFFF)
        return carry

    lax.fori_loop(0, n_chunks, score_body, 0)

    def count(pred):
        def count_body(c, cnt):
            for u in range(KCHUNK):
                kb = c * KCHUNK + u
                keys = key_ref[pl.ds(pl.multiple_of(kb * sub, sub), sub)]
                cnt = cnt + jnp.sum(pred(keys, kb).astype(jnp.int32), axis=0)
            return cnt

        cnt = lax.fori_loop(0, n_chunks, count_body, jnp.zeros((8, QB), jnp.int32))
        return jnp.sum(cnt, axis=0, keepdims=True)

    def rows(v):
        return jnp.broadcast_to(v, (8, QB))[None]

    n_keys = n_kb * QB
    for visited in range(1, n_kb // KCHUNK + 1):
        @pl.when(n_chunks == visited)
        def _(visited=visited):
            def bit_body(it, state):
                t, cnt_t = state
                cand = t + jnp.left_shift(1, 31 - it)
                cand_b = rows(cand)
                cnt = jnp.zeros((8, QB), jnp.int32)
                for c in range(visited):
                    keys = key_ref[c * KCHUNK * sub:(c + 1) * KCHUNK * sub]
                    cnt = cnt + jnp.sum((keys >= cand_b).astype(jnp.int32), axis=0)
                total = jnp.sum(cnt, axis=0, keepdims=True)
                take = total >= topk
                return jnp.where(take, cand, t), jnp.where(take, total, cnt_t)

            thr, cnt_thr = lax.fori_loop(
                0, 32, bit_body,
                (jnp.full((1, QB), INT_MIN, jnp.int32),
                 jnp.full((1, QB), n_keys + 1, jnp.int32)))
            thr_ref[...] = thr
            cnt_ref[...] = cnt_thr

    thr, cnt_thr = thr_ref[...], cnt_ref[...]
    thr_b = rows(thr)

    cut_ref[...] = jnp.full((1, QB), n_keys, jnp.int32)

    @pl.when(jnp.max(cnt_thr) > topk)
    def _():
        need = topk - count(lambda keys, kb: keys > thr_b)

        def pos_body(j, u):
            cand = u + jnp.left_shift(1, n_keys.bit_length() - 2 - j)
            cand_b = rows(cand)
            below = count(lambda keys, kb: (keys == thr_b) & (kb * QB + key_pos < cand_b))
            return jnp.where(below < need, cand, u)

        cut_ref[...] = lax.fori_loop(0, n_keys.bit_length() - 1, pos_body,
                                     jnp.zeros((1, QB), jnp.int32))

    cut_b = rows(cut_ref[...])

    def mask_body(kb, carry):
        keys = key_ref[pl.ds(pl.multiple_of(kb * sub, sub), sub)]
        sel = (keys > thr_b) | ((keys == thr_b) & (kb * QB + key_pos <= cut_b))
        sel = sel & ((kb < i) | diag_vis)
        mask_ref[0, 0, kb] = jnp.where(sel, 0.0, NEG).reshape(QB, QB).astype(mask_ref.dtype)
        return carry

    lax.fori_loop(0, n_vis, mask_body, 0)

    def zero_body(kb, carry):
        mask_ref[0, 0, kb] = jnp.full((QB, QB), NEG, mask_ref.dtype)
        return carry

    lax.fori_loop(n_vis, n_kb, zero_body, 0)


def _indexer_mask(idx, topk):
    B, S, _ = idx.shape
    n_kb = S // QB
    return pl.pallas_call(
        functools.partial(_indexer_kernel, n_kb=n_kb, topk=topk),
        grid=(B, n_kb),
        in_specs=[pl.BlockSpec((1, QB, IDX_PAD), lambda b, i: (b, i, 0)),
                  pl.BlockSpec((1, S, LANES), lambda b, i: (b, 0, KI_COL // LANES))],
        out_specs=pl.BlockSpec((1, 1, n_kb, QB, QB), lambda b, i: (b, i, 0, 0, 0)),
        out_shape=jax.ShapeDtypeStruct((B, n_kb, n_kb, QB, QB), BF16),
        scratch_shapes=[pltpu.VMEM((N_IDX_HEADS * QB, IDX_DIM), BF16),
                        pltpu.VMEM((N_IDX_HEADS, 8, QB), F32),
                        pltpu.VMEM((n_kb * QB // 8, 8, QB), jnp.int32)]
        + [pltpu.VMEM((1, QB), jnp.int32)] * 3,
        compiler_params=_params(("parallel", "parallel")),
        name="indexer_topk_mask",
    )(idx, idx)


def _sparse_attn_kernel(q_ref, k_ref, vt_ref, bias_ref, mask_ref, o_ref, m_ref, acc_ref, *,
                        n_tiles):
    i = pl.program_id(2)
    q = q_ref[0, 0]
    m_ref[...] = jnp.full(m_ref.shape, NEG, F32)
    acc_ref[...] = jnp.zeros(acc_ref.shape, F32)
    ones = jnp.ones((ONES_ROWS, KTILE), BF16)
    q_blocks = QTILE // QB
    k_blocks = KTILE // QB

    def step(c, tile):
        s = lax.dot_general(k_ref[0, 0, c * KTILE:(c + 1) * KTILE, :], q, (((1,), (1,)), ((), ())),
                            preferred_element_type=F32)
        rows = []
        for t in range(k_blocks):
            parts = []
            for a in range(q_blocks):
                bias = bias_ref[0, max(tile * q_blocks + a - (c * k_blocks + t), 0)]
                drop = mask_ref[0, a, c * k_blocks + t].astype(F32)
                parts.append(s[t * QB:(t + 1) * QB, a * QB:(a + 1) * QB] + bias + drop)
            rows.append(jnp.concatenate(parts, axis=1))
        s = jnp.concatenate(rows, axis=0)
        m_old = m_ref[...]
        m_new = jnp.maximum(m_old, jnp.max(s, axis=0, keepdims=True))
        alpha = jnp.exp2(m_old - m_new)
        p = jnp.exp2(s - m_new)
        m_ref[...] = m_new
        vt = [vt_ref[0, 0, c * (KTILE // QT) + j] for j in range(KTILE // QT)]
        v_ones = jnp.concatenate([jnp.concatenate(vt, axis=1), ones], axis=0)
        acc_ref[...] = alpha * acc_ref[...] + jnp.dot(v_ones, p.astype(BF16),
                                                      preferred_element_type=F32)

    for tile in range(n_tiles):
        @pl.when(i == tile)
        def _(tile=tile):
            for c in range((tile + 1) * (QTILE // KTILE)):
                step(c, tile)

    out = acc_ref[0:HEAD_DIM, :] / acc_ref[HEAD_DIM:HEAD_DIM + 1, :]
    o_ref[0] = out.T.astype(o_ref.dtype)


def _sparse_attention(qk, vt, toep_bias, mask):
    B, _, S, _ = qk.shape
    n_kb = S // QB
    return pl.pallas_call(
        functools.partial(_sparse_attn_kernel, n_tiles=S // QTILE),
        grid=(B, N_HEADS_B, S // QTILE),
        in_specs=[pl.BlockSpec((1, 1, QTILE, HEAD_DIM), lambda b, h, i: (b, h, i, 0)),
                  pl.BlockSpec((1, 1, S, HEAD_DIM), lambda b, h, i: (b, N_HEADS_B + h, 0, 0)),
                  pl.BlockSpec((1, 1, S // QT, HEAD_DIM, QT), lambda b, h, i: (b, h, 0, 0, 0)),
                  pl.BlockSpec((1, n_kb, QB, QB), lambda b, h, i: (h, 0, 0, 0)),
                  pl.BlockSpec((1, QTILE // QB, n_kb, QB, QB), lambda b, h, i: (b, i, 0, 0, 0))],
        out_specs=pl.BlockSpec((1, QTILE, HEAD_DIM), lambda b, h, i: (b, i, h)),
        out_shape=jax.ShapeDtypeStruct((B, S, WIDTH_B), BF16),
        scratch_shapes=[pltpu.VMEM((1, QTILE), F32),
                        pltpu.VMEM((HEAD_DIM + ONES_ROWS, QTILE), F32)],
        compiler_params=_params(("parallel", "parallel", "parallel")),
        name="sparse_attn",
    )(qk, qk, vt, toep_bias, mask)


def _merge_kernel(ya_ref, yb_ref, ga_ref, gb_ref, wa_ref, wb_ref, out_ref):
    a = jnp.dot(ya_ref[...], wa_ref[...].astype(BF16), preferred_element_type=F32)
    b = jnp.dot(yb_ref[...], wb_ref[...].astype(BF16), preferred_element_type=F32)
    out_ref[...] = (ga_ref[...].astype(F32) * a + gb_ref[...].astype(F32) * b).astype(out_ref.dtype)


def _merge(ya, yb, gates, wa, wb, *, tm, tn):
    M = yb.shape[0]
    N = wa.cols
    return pl.pallas_call(
        _merge_kernel,
        grid=(M // tm, N // tn),
        in_specs=[pl.BlockSpec((tm, WIDTH_A_OUT), lambda i, j: (i, 0)),
                  pl.BlockSpec((tm, WIDTH_B), lambda i, j: (i, 0)),
                  pl.BlockSpec((tm, tn), lambda i, j: (i, j)),
                  pl.BlockSpec((tm, tn), lambda i, j: (i, j + N // tn)),
                  wa.spec(tn), wb.spec(tn)],
        out_specs=pl.BlockSpec((tm, tn), lambda i, j: (i, j)),
        out_shape=jax.ShapeDtypeStruct((M, N), BF16),
        compiler_params=_params(("parallel", "parallel")),
        name="gated_merge",
    )(ya, yb, gates, gates, wa.stacked, wb.stacked)


def _matmul_residual_kernel(a_ref, w_ref, r_ref, o_ref):
    o_ref[...] = r_ref[...] + jnp.dot(a_ref[...], w_ref[...].astype(BF16),
                                      preferred_element_type=F32)


def _matmul_residual(a, w, r, *, tm, tn, name):
    M, K = a.shape
    N = w.cols
    return pl.pallas_call(
        _matmul_residual_kernel,
        grid=(M // tm, N // tn),
        in_specs=[pl.BlockSpec((tm, K), lambda i, j: (i, 0)), w.spec(tn),
                  pl.BlockSpec((tm, tn), lambda i, j: (i, j))],
        out_specs=pl.BlockSpec((tm, tn), lambda i, j: (i, j)),
        out_shape=jax.ShapeDtypeStruct((M, N), F32),
        compiler_params=_params(("parallel", "parallel")),
        name=name,
    )(a, w.stacked, r)


def _residual_norm_kernel(a_ref, w_ref, r_ref, g_ref, x_ref, h_ref):
    x = r_ref[...] + jnp.dot(a_ref[...], w_ref[...].astype(BF16), preferred_element_type=F32)
    x_ref[...] = x
    y = x * lax.rsqrt(jnp.mean(x * x, axis=-1, keepdims=True) + RMS_EPS)
    h_ref[...] = (y * g_ref[...]).astype(h_ref.dtype)


def _matmul_residual_norm(a, w, r, g, *, tm, name):
    M, K = a.shape
    N = w.cols
    whole_rows = pl.BlockSpec((tm, N), lambda i, j: (i, 0))
    return pl.pallas_call(
        _residual_norm_kernel,
        grid=(M // tm, 1),
        in_specs=[pl.BlockSpec((tm, K), lambda i, j: (i, 0)), w.spec(N), whole_rows,
                  pl.BlockSpec((1, N), lambda i, j: (0, 0))],
        out_specs=[whole_rows, whole_rows],
        out_shape=[jax.ShapeDtypeStruct((M, N), F32), jax.ShapeDtypeStruct((M, N), BF16)],
        compiler_params=_params(("parallel", "arbitrary")),
        name=name,
    )(a, w.stacked, r, g.reshape(1, N))


def _swiglu_kernel(h_ref, wg_ref, wu_ref, o_ref):
    h = h_ref[...]
    gate = jnp.dot(h, wg_ref[...].astype(BF16), preferred_element_type=F32)
    up = jnp.dot(h, wu_ref[...].astype(BF16), preferred_element_type=F32)
    o_ref[...] = (gate * _sigmoid(gate) * up).astype(o_ref.dtype)


def _swiglu(h, wg, wu, *, tm, tn):
    M, D = h.shape
    N = wg.cols
    return pl.pallas_call(
        _swiglu_kernel,
        grid=(M // tm, N // tn),
        in_specs=[pl.BlockSpec((tm, D), lambda i, j: (i, 0)), wg.spec(tn), wu.spec(tn)],
        out_specs=pl.BlockSpec((tm, tn), lambda i, j: (i, j)),
        out_shape=jax.ShapeDtypeStruct((M, N), BF16),
        compiler_params=_params(("parallel", "parallel")),
        name="swiglu_gate_up",
    )(h, wg.stacked, wu.stacked)


def kernel(x, w_in, w_out_a, w_out_b, w_out, w_ff_gate, w_ff_up, w_ff_down,
           g_mix, g_ffn, g_final, rel_bias):
    B, S, D = x.shape
    depth = w_in.shape[0]
    M = B * S
    assert S % TS == 0 and S % QTILE == 0
    assert w_in.shape[2] == QKV_WIDTH + IDX_WIDTH + 2 * D
    topk = min(TOPK_MAX, S // 4)
    tm, tm_small, tn = 2048, 1024, 512

    band_bias, toep_bias = _bias_tiles(rel_bias, S // QB)
    x2 = x.reshape(M, D)
    h = _rmsnorm(x2, g_mix[0], tm=tm_small, out_dtype=BF16, name="rmsnorm_mix")
    for layer in range(depth):
        wl = w_in[layer]
        w_qkv = wl[:, :QKV_WIDTH].astype(BF16)[None]
        w_a = Weight(w_qkv, 0, 0, 3 * WIDTH_A)
        w_qkb = Weight(w_qkv, 0, 3 * WIDTH_A, 2 * WIDTH_B)
        w_vb = Weight(w_qkv, 0, QKV_WIDTH - WIDTH_B, WIDTH_B)
        w_idx = _whole(jnp.pad(wl[:, QKV_WIDTH:QKV_WIDTH + IDX_WIDTH],
                               ((0, 0), (0, IDX_PAD - IDX_WIDTH))).astype(BF16)[None], 0)
        w_gate = _whole(wl[:, QKV_WIDTH + IDX_WIDTH:].astype(BF16)[None], 0)

        qkv_a = _matmul(h, w_a, tm=tm, tn=tn, out_dtype=F32, layout="head",
                        batch_seq=(B, S), name="proj_a")
        qk_b = _matmul(h, w_qkb, tm=tm, tn=tn, out_dtype=BF16, layout="head", batch_seq=(B, S),
                       lead_cols=WIDTH_B, lead_scale=SPARSE_Q_SCALE, name="proj_qk_b")
        vt = _matmul(h, w_vb, tm=tm_small, tn=tn, out_dtype=BF16, layout="head_t",
                     batch_seq=(B, S), name="proj_vt")
        idx = _matmul(h, w_idx, tm=tm_small, tn=IDX_PAD, out_dtype=F32, name="proj_idx")
        gates = _matmul(h, w_gate, tm=tm, tn=tn, out_dtype=BF16, layout="sigmoid",
                        name="proj_gates")

        y_a = _dilated_attention(qkv_a, band_bias)
        mask = _indexer_mask(idx.reshape(B, S, IDX_PAD), topk)
        y_b = _sparse_attention(qk_b, vt, toep_bias, mask).reshape(M, WIDTH_B)

        merged = _merge(y_a, y_b, gates, _whole(w_out_a, layer), _whole(w_out_b, layer),
                        tm=tm, tn=tn)
        x2, h = _matmul_residual_norm(merged, _whole(w_out[layer].astype(BF16)[None], 0), x2,
                                      g_ffn[layer], tm=tm_small // 2, name="out_proj_norm")
        act = _swiglu(h, _whole(w_ff_gate, layer), _whole(w_ff_up, layer), tm=tm, tn=tn)
        x2 = _matmul_residual(act, _whole(w_ff_down[layer].astype(BF16)[None], 0), x2,
                              tm=tm_small, tn=tn, name="ffn_down")
        if layer + 1 < depth:
            h = _rmsnorm(x2, g_mix[layer + 1], tm=tm_small, out_dtype=BF16, name="rmsnorm_mix")
    return _rmsnorm(x2, g_final, tm=tm_small, out_dtype=F32, name="final_rmsnorm").reshape(B, S, D)
```

```python
import functools
import math
from typing import NamedTuple

import jax
import jax.numpy as jnp
from jax import lax
from jax.experimental import pallas as pl
from jax.experimental.pallas import tpu as pltpu

F32 = jnp.float32
BF16 = jnp.bfloat16

HEAD_DIM = 128
DILATIONS = (1, 4, 16)
BAND = 128
HEADS_PER_GROUP = 4
N_HEADS_A = 12
N_HEADS_B = 8
WIDTH_A = N_HEADS_A * HEAD_DIM
WIDTH_A_OUT = HEADS_PER_GROUP * HEAD_DIM
WIDTH_B = N_HEADS_B * HEAD_DIM
N_IDX_HEADS = 16
IDX_DIM = 64
TOPK_MAX = 256
REL_BUCKETS = 32
REL_MAX_DIST = 2048
RMS_EPS = 1e-6
QKV_WIDTH = 3 * WIDTH_A + 3 * WIDTH_B
IDX_WIDTH = N_IDX_HEADS * IDX_DIM + IDX_DIM + N_IDX_HEADS
IDX_PAD = 1152
KI_COL = N_IDX_HEADS * IDX_DIM
WI_COL = KI_COL + IDX_DIM
LANES = 128
QB = 128
KCHUNK = 4
QT = KCHUNK * QB
QTILE = 2 * QT
KTILE = 2 * QT
TS = BAND * max(DILATIONS)
NEG = -(2.0 ** 100)
INT_MIN = -(2 ** 31)
LOG2E = math.log2(math.e)
SPARSE_Q_SCALE = HEAD_DIM ** -0.5 * LOG2E
ONES_ROWS = 16
VMEM_LIMIT = 56 * 1024 * 1024


def _params(sem):
    return pltpu.CompilerParams(dimension_semantics=sem, vmem_limit_bytes=VMEM_LIMIT)


def _sigmoid(x):
    return 0.5 * jnp.tanh(0.5 * x) + 0.5


class Weight(NamedTuple):
    stacked: jax.Array
    layer: int
    col0: int
    cols: int

    def spec(self, tn):
        assert self.col0 % tn == 0 and self.cols % tn == 0
        first = self.col0 // tn
        return pl.BlockSpec((None, self.stacked.shape[1], tn),
                            lambda i, j, layer=self.layer: (layer, 0, first + j))


def _whole(stacked, layer):
    return Weight(stacked, layer, 0, stacked.shape[2])


def _rmsnorm_kernel(x_ref, g_ref, o_ref):
    x = x_ref[...]
    y = x * lax.rsqrt(jnp.mean(x * x, axis=-1, keepdims=True) + RMS_EPS)
    o_ref[...] = (y * g_ref[...]).astype(o_ref.dtype)


def _rmsnorm(x2, g, *, tm, out_dtype, name):
    M, D = x2.shape
    return pl.pallas_call(
        _rmsnorm_kernel,
        grid=(M // tm,),
        in_specs=[pl.BlockSpec((tm, D), lambda i: (i, 0)), pl.BlockSpec((1, D), lambda i: (0, 0))],
        out_specs=pl.BlockSpec((tm, D), lambda i: (i, 0)),
        out_shape=jax.ShapeDtypeStruct((M, D), out_dtype),
        compiler_params=_params(("parallel",)),
        name=name,
    )(x2, g.reshape(1, D))


def _matmul_kernel(a_ref, w_ref, o_ref, *, layout, lead_tiles, lead_scale):
    acc = jnp.dot(a_ref[...], w_ref[...].astype(BF16), preferred_element_type=F32)
    if lead_tiles:
        acc = acc * jnp.where(pl.program_id(1) < lead_tiles, lead_scale, 1.0)
    if layout == "rows":
        o_ref[...] = acc.astype(o_ref.dtype)
    elif layout == "sigmoid":
        o_ref[...] = _sigmoid(acc).astype(o_ref.dtype)
    else:
        for c in range(acc.shape[1] // HEAD_DIM):
            head = acc[:, c * HEAD_DIM:(c + 1) * HEAD_DIM]
            if layout == "head":
                o_ref[0, c] = head.astype(o_ref.dtype)
            else:
                for r in range(acc.shape[0] // QT):
                    o_ref[0, c, r] = head[r * QT:(r + 1) * QT].T.astype(o_ref.dtype)


def _matmul(a, w, *, tm, tn, out_dtype, layout="rows", batch_seq=None, lead_cols=0,
            lead_scale=1.0, name):
    assert lead_cols % tn == 0
    M, K = a.shape
    N = w.cols
    if layout in ("rows", "sigmoid"):
        out_shape = jax.ShapeDtypeStruct((M, N), out_dtype)
        out_spec = pl.BlockSpec((tm, tn), lambda i, j: (i, j))
    else:
        B, S = batch_seq
        per_b = S // tm
        if layout == "head":
            out_shape = jax.ShapeDtypeStruct((B, N // HEAD_DIM, S, HEAD_DIM), out_dtype)
            out_spec = pl.BlockSpec((1, tn // HEAD_DIM, tm, HEAD_DIM),
                                    lambda i, j: (i // per_b, j, i % per_b, 0))
        else:
            out_shape = jax.ShapeDtypeStruct((B, N // HEAD_DIM, S // QT, HEAD_DIM, QT), out_dtype)
            out_spec = pl.BlockSpec((1, tn // HEAD_DIM, tm // QT, HEAD_DIM, QT),
                                    lambda i, j: (i // per_b, j, i % per_b, 0, 0))
    return pl.pallas_call(
        functools.partial(_matmul_kernel, layout=layout, lead_tiles=lead_cols // tn,
                          lead_scale=lead_scale),
        grid=(M // tm, N // tn),
        in_specs=[pl.BlockSpec((tm, K), lambda i, j: (i, 0)), w.spec(tn)],
        out_specs=out_spec,
        out_shape=out_shape,
        compiler_params=_params(("parallel", "parallel")),
        name=name,
    )(a, w.stacked)


def _t5_bucket(dist):
    n = jnp.maximum(dist, 0)
    max_exact = REL_BUCKETS // 2
    nf = jnp.maximum(n, 1).astype(F32)
    large = max_exact + (jnp.log(nf / max_exact) / math.log(REL_MAX_DIST / max_exact)
                         * (REL_BUCKETS - max_exact)).astype(jnp.int32)
    large = jnp.minimum(large, REL_BUCKETS - 1)
    return jnp.where(n < max_exact, n, large)


def _table_lookup(tab_ref, bucket, col):
    val = jnp.zeros(bucket.shape, F32)
    for b in range(REL_BUCKETS):
        val = jnp.where(bucket == b, tab_ref[b, col], val)
    return val


def _band_bias_kernel(tab_ref, o_ref):
    h = pl.program_id(0)
    dilation = jnp.left_shift(1, 2 * (h // HEADS_PER_GROUP))
    qi = lax.broadcasted_iota(jnp.int32, (BAND, 2 * BAND), 0)
    kj = lax.broadcasted_iota(jnp.int32, (BAND, 2 * BAND), 1)
    delta = qi + BAND - kj
    bias = _table_lookup(tab_ref, _t5_bucket(delta * dilation), h)
    o_ref[0] = jnp.where((delta >= 0) & (delta <= BAND), bias, -jnp.inf)


def _toeplitz_bias_kernel(tab_ref, o_ref):
    h = pl.program_id(0)
    kj = lax.broadcasted_iota(jnp.int32, (QB, QB), 0)
    qi = lax.broadcasted_iota(jnp.int32, (QB, QB), 1)

    def body(d, carry):
        dist = d * QB + qi - kj
        o_ref[0, d] = _table_lookup(tab_ref, _t5_bucket(dist), N_HEADS_A + h) * LOG2E
        return carry

    lax.fori_loop(0, o_ref.shape[1], body, 0)


def _bias_tiles(rel_bias, n_blocks):
    smem = pl.BlockSpec(memory_space=pltpu.SMEM)
    band = pl.pallas_call(
        _band_bias_kernel,
        grid=(N_HEADS_A,),
        in_specs=[smem],
        out_specs=pl.BlockSpec((1, BAND, 2 * BAND), lambda h: (h, 0, 0)),
        out_shape=jax.ShapeDtypeStruct((N_HEADS_A, BAND, 2 * BAND), F32),
        name="band_bias",
    )(rel_bias)
    toep = pl.pallas_call(
        _toeplitz_bias_kernel,
        grid=(N_HEADS_B,),
        in_specs=[smem],
        out_specs=pl.BlockSpec((1, n_blocks, QB, QB), lambda h: (h, 0, 0, 0)),
        out_shape=jax.ShapeDtypeStruct((N_HEADS_B, n_blocks, QB, QB), F32),
        name="toeplitz_bias",
    )(rel_bias)
    return band, toep


def _dilated_kernel(*refs):
    n_g = len(DILATIONS)
    groups = [refs[6 * g:6 * g + 6] for g in range(n_g)]
    y_ref, o_scr, l_scr = refs[6 * n_g:]
    i = pl.program_id(2)
    scale = HEAD_DIM ** -0.5
    col = lax.broadcasted_iota(jnp.int32, (BAND, 2 * BAND), 1)

    for g, d in enumerate(DILATIONS):
        q_ref, k_ref, v_ref, kp_ref, vp_ref, bias_ref = groups[g]
        span = d * BAND
        bias = bias_ref[0]
        bias_first = jnp.where((i > 0) | (col >= BAND), bias, -jnp.inf)
        for r in range(d):
            prev = pl.ds(r, BAND, stride=d)
            k_prev, v_prev = kp_ref[0, 0, prev, :].astype(BF16), vp_ref[0, 0, prev, :].astype(BF16)
            for n in range(TS // span):
                rows = pl.ds(n * span + r, BAND, stride=d)
                b = bias_first if n == 0 else bias
                k_cur, v_cur = k_ref[0, 0, rows, :].astype(BF16), v_ref[0, 0, rows, :].astype(BF16)
                kwin = jnp.concatenate([k_prev, k_cur], axis=0)
                vwin = jnp.concatenate([v_prev, v_cur], axis=0)
                k_prev, v_prev = k_cur, v_cur
                q = q_ref[0, 0, rows, :].astype(BF16)
                s = lax.dot_general(q, kwin, (((1,), (1,)), ((), ())), preferred_element_type=F32)
                s = s * scale + b
                m = jnp.max(s, axis=-1, keepdims=True)
                p = jnp.exp(s - m)
                l = jnp.sum(p, axis=-1, keepdims=True)
                o_scr[g, rows, :] = jnp.dot(p.astype(BF16), vwin, preferred_element_type=F32) / l
                l_scr[g, rows, :] = jnp.broadcast_to(m + jnp.log(l), (BAND, HEAD_DIM))

    lses = [l_scr[g] for g in range(n_g)]
    mx = functools.reduce(jnp.maximum, lses)
    es = [jnp.exp(lse - mx) for lse in lses]
    tot = functools.reduce(jnp.add, es)
    y = functools.reduce(jnp.add, [(e / tot) * o_scr[g] for g, e in enumerate(es)])
    y_ref[...] = y.astype(y_ref.dtype)


def _dilated_attention(qkv_a, band_bias):
    B, _, S, _ = qkv_a.shape
    tiles = S // TS
    in_specs, args = [], []
    for g, d in enumerate(DILATIONS):
        span = d * BAND
        per_tile = TS // span
        for base in (0, N_HEADS_A, 2 * N_HEADS_A):
            in_specs.append(pl.BlockSpec(
                (1, 1, TS, HEAD_DIM),
                lambda b, h, i, hd=base + g * HEADS_PER_GROUP: (b, hd + h, i, 0)))
        for base in (N_HEADS_A, 2 * N_HEADS_A):
            in_specs.append(pl.BlockSpec(
                (1, 1, span, HEAD_DIM),
                lambda b, h, i, hd=base + g * HEADS_PER_GROUP, pt=per_tile:
                (b, hd + h, jnp.maximum(i * pt - 1, 0), 0)))
        in_specs.append(pl.BlockSpec((1, BAND, 2 * BAND),
                                     lambda b, h, i, g=g: (g * HEADS_PER_GROUP + h, 0, 0)))
        args += [qkv_a] * 5 + [band_bias]
    return pl.pallas_call(
        _dilated_kernel,
        grid=(B, HEADS_PER_GROUP, tiles),
        in_specs=in_specs,
        out_specs=pl.BlockSpec((TS, HEAD_DIM), lambda b, h, i: (b * tiles + i, h)),
        out_shape=jax.ShapeDtypeStruct((B * S, WIDTH_A_OUT), BF16),
        scratch_shapes=[pltpu.VMEM((len(DILATIONS), TS, HEAD_DIM), F32),
                        pltpu.VMEM((len(DILATIONS), TS, HEAD_DIM), F32)],
        compiler_params=_params(("parallel", "parallel", "parallel")),
        name="dilated_attn",
    )(*args)


def _indexer_kernel(q_ref, k_ref, mask_ref, qh_ref, wb_ref, key_ref, cut_ref, thr_ref, cnt_ref,
                    *, n_kb, topk):
    i = pl.program_id(1)
    n_vis = i + 1
    n_chunks = (i + KCHUNK) // KCHUNK
    sub = QB // 8
    for h in range(N_IDX_HEADS):
        qh_ref[h * QB:(h + 1) * QB, :] = q_ref[0, :, h * IDX_DIM:(h + 1) * IDX_DIM].astype(BF16)
    wt = q_ref[0, :, KI_COL:KI_COL + LANES].T * ((N_IDX_HEADS ** -0.5) * (IDX_DIM ** -0.5))
    for h in range(N_IDX_HEADS):
        wb_ref[h] = jnp.broadcast_to(wt[IDX_DIM + h:IDX_DIM + h + 1, :], (8, QB))

    key_pos = (lax.broadcasted_iota(jnp.int32, (sub, 8, QB), 0) * 8
               + lax.broadcasted_iota(jnp.int32, (sub, 8, QB), 1))
    diag_vis = key_pos <= lax.broadcasted_iota(jnp.int32, (sub, 8, QB), 2)

    def score_chunk(c):
        kchunk = k_ref[0, c * QT:(c + 1) * QT, 0:IDX_DIM].astype(BF16)
        accs = [jnp.zeros((sub, 8, QB), F32) for _ in range(KCHUNK)]
        for pair in range(N_IDX_HEADS // 2):
            d = lax.dot_general(kchunk, qh_ref[pair * 2 * QB:(pair + 1) * 2 * QB, :],
                                (((1,), (1,)), ((), ())), preferred_element_type=F32)
            for u in range(2):
                w = wb_ref[2 * pair + u][None]
                for t in range(KCHUNK):
                    tile = d[t * QB:(t + 1) * QB, u * QB:(u + 1) * QB].reshape(sub, 8, QB)
                    accs[t] = accs[t] + jnp.maximum(tile, 0.0) * w
        for t in range(KCHUNK):
            kb = c * KCHUNK + t
            vis = (kb < i) | ((kb == i) & diag_vis)
            bits = pltpu.bitcast(jnp.where(vis, accs[t], -jnp.inf), jnp.int32)
            key_ref[kb * sub:(kb + 1) * sub] = bits ^ ((bits >> 31) & 0x7FFFFFFF)

    def count(pred):
        def count_body(c, cnt):
            for u in range(KCHUNK):
                kb = c * KCHUNK + u
                keys = key_ref[pl.ds(pl.multiple_of(kb * sub, sub), sub)]
                cnt = cnt + jnp.sum(pred(keys, kb).astype(jnp.int32), axis=0)
            return cnt

        cnt = lax.fori_loop(0, n_chunks, count_body, jnp.zeros((8, QB), jnp.int32))
        return jnp.sum(cnt, axis=0, keepdims=True)

    def rows(v):
        return jnp.broadcast_to(v, (8, QB))[None]

    n_keys = n_kb * QB
    for visited in range(1, n_kb // KCHUNK + 1):
        @pl.when(n_chunks == visited)
        def _(visited=visited):
            for c in range(visited):
                score_chunk(c)

            def bit_body(it, state):
                t, cnt_t = state
                cand = t + jnp.left_shift(1, 31 - it)
                cand_b = rows(cand)
                cnt = jnp.zeros((8, QB), jnp.int32)
                for c in range(visited):
                    keys = key_ref[c * KCHUNK * sub:(c + 1) * KCHUNK * sub]
                    cnt = cnt + jnp.sum((keys >= cand_b).astype(jnp.int32), axis=0)
                total = jnp.sum(cnt, axis=0, keepdims=True)
                take = total >= topk
                return jnp.where(take, cand, t), jnp.where(take, total, cnt_t)

            thr, cnt_thr = lax.fori_loop(
                0, 32, bit_body,
                (jnp.full((1, QB), INT_MIN, jnp.int32),
                 jnp.full((1, QB), n_keys + 1, jnp.int32)))
            thr_ref[...] = thr
            cnt_ref[...] = cnt_thr

    thr, cnt_thr = thr_ref[...], cnt_ref[...]
    thr_b = rows(thr)

    cut_ref[...] = jnp.full((1, QB), n_keys, jnp.int32)

    @pl.when(jnp.max(cnt_thr) > topk)
    def _():
        need = topk - count(lambda keys, kb: keys > thr_b)

        def pos_body(j, u):
            cand = u + jnp.left_shift(1, n_keys.bit_length() - 2 - j)
            cand_b = rows(cand)
            below = count(lambda keys, kb: (keys == thr_b) & (kb * QB + key_pos < cand_b))
            return jnp.where(below < need, cand, u)

        cut_ref[...] = lax.fori_loop(0, n_keys.bit_length() - 1, pos_body,
                                     jnp.zeros((1, QB), jnp.int32))

    cut_b = rows(cut_ref[...])

    def mask_body(kb, carry):
        keys = key_ref[pl.ds(pl.multiple_of(kb * sub, sub), sub)]
        sel = (keys > thr_b) | ((keys == thr_b) & (kb * QB + key_pos <= cut_b))
        sel = sel & ((kb < i) | diag_vis)
        mask_ref[0, 0, kb] = jnp.where(sel, 0.0, NEG).reshape(QB, QB).astype(mask_ref.dtype)
        return carry

    lax.fori_loop(0, n_vis, mask_body, 0)

    def zero_body(kb, carry):
        mask_ref[0, 0, kb] = jnp.full((QB, QB), NEG, mask_ref.dtype)
        return carry

    lax.fori_loop(n_vis, n_kb, zero_body, 0)


def _indexer_mask(idx, topk):
    B, S, _ = idx.shape
    n_kb = S // QB
    return pl.pallas_call(
        functools.partial(_indexer_kernel, n_kb=n_kb, topk=topk),
        grid=(B, n_kb),
        in_specs=[pl.BlockSpec((1, QB, IDX_PAD), lambda b, i: (b, i, 0)),
                  pl.BlockSpec((1, S, LANES), lambda b, i: (b, 0, KI_COL // LANES))],
        out_specs=pl.BlockSpec((1, 1, n_kb, QB, QB), lambda b, i: (b, i, 0, 0, 0)),
        out_shape=jax.ShapeDtypeStruct((B, n_kb, n_kb, QB, QB), BF16),
        scratch_shapes=[pltpu.VMEM((N_IDX_HEADS * QB, IDX_DIM), BF16),
                        pltpu.VMEM((N_IDX_HEADS, 8, QB), F32),
                        pltpu.VMEM((n_kb * QB // 8, 8, QB), jnp.int32)]
        + [pltpu.VMEM((1, QB), jnp.int32)] * 3,
        compiler_params=_params(("parallel", "parallel")),
        name="indexer_topk_mask",
    )(idx, idx)


def _sparse_attn_kernel(q_ref, k_ref, vt_ref, bias_ref, mask_ref, o_ref, m_ref, acc_ref, *,
                        n_tiles):
    i = pl.program_id(2)
    q = q_ref[0, 0]
    m_ref[...] = jnp.full(m_ref.shape, NEG, F32)
    acc_ref[...] = jnp.zeros(acc_ref.shape, F32)
    ones = jnp.ones((ONES_ROWS, KTILE), BF16)
    q_blocks = QTILE // QB
    k_blocks = KTILE // QB

    def step(c, tile):
        s = lax.dot_general(k_ref[0, 0, c * KTILE:(c + 1) * KTILE, :], q, (((1,), (1,)), ((), ())),
                            preferred_element_type=F32)
        rows = []
        for t in range(k_blocks):
            parts = []
            for a in range(q_blocks):
                bias = bias_ref[0, max(tile * q_blocks + a - (c * k_blocks + t), 0)]
                drop = mask_ref[0, a, c * k_blocks + t].astype(F32)
                parts.append(s[t * QB:(t + 1) * QB, a * QB:(a + 1) * QB] + bias + drop)
            rows.append(jnp.concatenate(parts, axis=1))
        s = jnp.concatenate(rows, axis=0)
        m_old = m_ref[...]
        m_new = jnp.maximum(m_old, jnp.max(s, axis=0, keepdims=True))
        alpha = jnp.exp2(m_old - m_new)
        p = jnp.exp2(s - m_new)
        m_ref[...] = m_new
        vt = [vt_ref[0, 0, c * (KTILE // QT) + j] for j in range(KTILE // QT)]
        v_ones = jnp.concatenate([jnp.concatenate(vt, axis=1), ones], axis=0)
        acc_ref[...] = alpha * acc_ref[...] + jnp.dot(v_ones, p.astype(BF16),
                                                      preferred_element_type=F32)

    for tile in range(n_tiles):
        @pl.when(i == tile)
        def _(tile=tile):
            for c in range((tile + 1) * (QTILE // KTILE)):
                step(c, tile)

    out = acc_ref[0:HEAD_DIM, :] / acc_ref[HEAD_DIM:HEAD_DIM + 1, :]
    o_ref[0] = out.T.astype(o_ref.dtype)


def _sparse_attention(qk, vt, toep_bias, mask):
    B, _, S, _ = qk.shape
    n_kb = S // QB
    return pl.pallas_call(
        functools.partial(_sparse_attn_kernel, n_tiles=S // QTILE),
        grid=(B, N_HEADS_B, S // QTILE),
        in_specs=[pl.BlockSpec((1, 1, QTILE, HEAD_DIM), lambda b, h, i: (b, h, i, 0)),
                  pl.BlockSpec((1, 1, S, HEAD_DIM), lambda b, h, i: (b, N_HEADS_B + h, 0, 0)),
                  pl.BlockSpec((1, 1, S // QT, HEAD_DIM, QT), lambda b, h, i: (b, h, 0, 0, 0)),
                  pl.BlockSpec((1, n_kb, QB, QB), lambda b, h, i: (h, 0, 0, 0)),
                  pl.BlockSpec((1, QTILE // QB, n_kb, QB, QB), lambda b, h, i: (b, i, 0, 0, 0))],
        out_specs=pl.BlockSpec((1, QTILE, HEAD_DIM), lambda b, h, i: (b, i, h)),
        out_shape=jax.ShapeDtypeStruct((B, S, WIDTH_B), BF16),
        scratch_shapes=[pltpu.VMEM((1, QTILE), F32),
                        pltpu.VMEM((HEAD_DIM + ONES_ROWS, QTILE), F32)],
        compiler_params=_params(("parallel", "parallel", "parallel")),
        name="sparse_attn",
    )(qk, qk, vt, toep_bias, mask)


def _merge_kernel(ya_ref, yb_ref, ga_ref, gb_ref, wa_ref, wb_ref, out_ref):
    a = jnp.dot(ya_ref[...], wa_ref[...].astype(BF16), preferred_element_type=F32)
    b = jnp.dot(yb_ref[...], wb_ref[...].astype(BF16), preferred_element_type=F32)
    out_ref[...] = (ga_ref[...].astype(F32) * a + gb_ref[...].astype(F32) * b).astype(out_ref.dtype)


def _merge(ya, yb, gates, wa, wb, *, tm, tn):
    M = yb.shape[0]
    N = wa.cols
    return pl.pallas_call(
        _merge_kernel,
        grid=(M // tm, N // tn),
        in_specs=[pl.BlockSpec((tm, WIDTH_A_OUT), lambda i, j: (i, 0)),
                  pl.BlockSpec((tm, WIDTH_B), lambda i, j: (i, 0)),
                  pl.BlockSpec((tm, tn), lambda i, j: (i, j)),
                  pl.BlockSpec((tm, tn), lambda i, j: (i, j + N // tn)),
                  wa.spec(tn), wb.spec(tn)],
        out_specs=pl.BlockSpec((tm, tn), lambda i, j: (i, j)),
        out_shape=jax.ShapeDtypeStruct((M, N), BF16),
        compiler_params=_params(("parallel", "parallel")),
        name="gated_merge",
    )(ya, yb, gates, gates, wa.stacked, wb.stacked)


def _matmul_residual_kernel(a_ref, w_ref, r_ref, o_ref):
    o_ref[...] = r_ref[...] + jnp.dot(a_ref[...], w_ref[...].astype(BF16),
                                      preferred_element_type=F32)


def _matmul_residual(a, w, r, *, tm, tn, name):
    M, K = a.shape
    N = w.cols
    return pl.pallas_call(
        _matmul_residual_kernel,
        grid=(M // tm, N // tn),
        in_specs=[pl.BlockSpec((tm, K), lambda i, j: (i, 0)), w.spec(tn),
                  pl.BlockSpec((tm, tn), lambda i, j: (i, j))],
        out_specs=pl.BlockSpec((tm, tn), lambda i, j: (i, j)),
        out_shape=jax.ShapeDtypeStruct((M, N), F32),
        compiler_params=_params(("parallel", "parallel")),
        name=name,
    )(a, w.stacked, r)


def _residual_norm_kernel(a_ref, w_ref, r_ref, g_ref, x_ref, h_ref):
    x = r_ref[...] + jnp.dot(a_ref[...], w_ref[...].astype(BF16), preferred_element_type=F32)
    x_ref[...] = x
    y = x * lax.rsqrt(jnp.mean(x * x, axis=-1, keepdims=True) + RMS_EPS)
    h_ref[...] = (y * g_ref[...]).astype(h_ref.dtype)


def _matmul_residual_norm(a, w, r, g, *, tm, name):
    M, K = a.shape
    N = w.cols
    whole_rows = pl.BlockSpec((tm, N), lambda i, j: (i, 0))
    return pl.pallas_call(
        _residual_norm_kernel,
        grid=(M // tm, 1),
        in_specs=[pl.BlockSpec((tm, K), lambda i, j: (i, 0)), w.spec(N), whole_rows,
                  pl.BlockSpec((1, N), lambda i, j: (0, 0))],
        out_specs=[whole_rows, whole_rows],
        out_shape=[jax.ShapeDtypeStruct((M, N), F32), jax.ShapeDtypeStruct((M, N), BF16)],
        compiler_params=_params(("parallel", "arbitrary")),
        name=name,
    )(a, w.stacked, r, g.reshape(1, N))


def _swiglu_kernel(h_ref, wg_ref, wu_ref, o_ref):
    h = h_ref[...]
    gate = jnp.dot(h, wg_ref[...].astype(BF16), preferred_element_type=F32)
    up = jnp.dot(h, wu_ref[...].astype(BF16), preferred_element_type=F32)
    o_ref[...] = (gate * _sigmoid(gate) * up).astype(o_ref.dtype)


def _swiglu(h, wg, wu, *, tm, tn):
    M, D = h.shape
    N = wg.cols
    return pl.pallas_call(
        _swiglu_kernel,
        grid=(M // tm, N // tn),
        in_specs=[pl.BlockSpec((tm, D), lambda i, j: (i, 0)), wg.spec(tn), wu.spec(tn)],
        out_specs=pl.BlockSpec((tm, tn), lambda i, j: (i, j)),
        out_shape=jax.ShapeDtypeStruct((M, N), BF16),
        compiler_params=_params(("parallel", "parallel")),
        name="swiglu_gate_up",
    )(h, wg.stacked, wu.stacked)


def kernel(x, w_in, w_out_a, w_out_b, w_out, w_ff_gate, w_ff_up, w_ff_down,
           g_mix, g_ffn, g_final, rel_bias):
    B, S, D = x.shape
    depth = w_in.shape[0]
    M = B * S
    assert S % TS == 0 and S % QTILE == 0
    assert w_in.shape[2] == QKV_WIDTH + IDX_WIDTH + 2 * D
    topk = min(TOPK_MAX, S // 4)
    tm, tm_small, tn = 2048, 1024, 512

    band_bias, toep_bias = _bias_tiles(rel_bias, S // QB)
    x2 = x.reshape(M, D)
    h = _rmsnorm(x2, g_mix[0], tm=tm_small, out_dtype=BF16, name="rmsnorm_mix")
    for layer in range(depth):
        wl = w_in[layer]
        w_qkv = wl[:, :QKV_WIDTH].astype(BF16)[None]
        w_a = Weight(w_qkv, 0, 0, 3 * WIDTH_A)
        w_qkb = Weight(w_qkv, 0, 3 * WIDTH_A, 2 * WIDTH_B)
        w_vb = Weight(w_qkv, 0, QKV_WIDTH - WIDTH_B, WIDTH_B)
        w_idx = _whole(jnp.pad(wl[:, QKV_WIDTH:QKV_WIDTH + IDX_WIDTH],
                               ((0, 0), (0, IDX_PAD - IDX_WIDTH))).astype(BF16)[None], 0)
        w_gate = _whole(wl[:, QKV_WIDTH + IDX_WIDTH:].astype(BF16)[None], 0)

        qkv_a = _matmul(h, w_a, tm=tm, tn=tn, out_dtype=F32, layout="head",
                        batch_seq=(B, S), name="proj_a")
        qk_b = _matmul(h, w_qkb, tm=tm, tn=tn, out_dtype=BF16, layout="head", batch_seq=(B, S),
                       lead_cols=WIDTH_B, lead_scale=SPARSE_Q_SCALE, name="proj_qk_b")
        vt = _matmul(h, w_vb, tm=tm_small, tn=tn, out_dtype=BF16, layout="head_t",
                     batch_seq=(B, S), name="proj_vt")
        idx = _matmul(h, w_idx, tm=tm_small, tn=IDX_PAD, out_dtype=F32, name="proj_idx")
        gates = _matmul(h, w_gate, tm=tm, tn=tn, out_dtype=BF16, layout="sigmoid",
                        name="proj_gates")

        y_a = _dilated_attention(qkv_a, band_bias)
        mask = _indexer_mask(idx.reshape(B, S, IDX_PAD), topk)
        y_b = _sparse_attention(qk_b, vt, toep_bias, mask).reshape(M, WIDTH_B)

        merged = _merge(y_a, y_b, gates, _whole(w_out_a, layer), _whole(w_out_b, layer),
                        tm=tm, tn=tn)
        x2, h = _matmul_residual_norm(merged, _whole(w_out[layer].astype(BF16)[None], 0), x2,
                                      g_ffn[layer], tm=tm_small // 2, name="out_proj_norm")
        act = _swiglu(h, _whole(w_ff_gate, layer), _whole(w_ff_up, layer), tm=tm, tn=tn)
        x2 = _matmul_residual(act, _whole(w_ff_down[layer].astype(BF16)[None], 0), x2,
                              tm=tm_small, tn=tn, name="ffn_down")
        if layer + 1 < depth:
            h = _rmsnorm(x2, g_mix[layer + 1], tm=tm_small, out_dtype=BF16, name="rmsnorm_mix")
    return _rmsnorm(x2, g_final, tm=tm_small, out_dtype=F32, name="final_rmsnorm").reshape(B, S, D)
```

```python
import functools
import math
from typing import NamedTuple

import jax
import jax.numpy as jnp
from jax import lax
from jax.experimental import pallas as pl
from jax.experimental.pallas import tpu as pltpu

F32 = jnp.float32
BF16 = jnp.bfloat16

HEAD_DIM = 128
DILATIONS = (1, 4, 16)
BAND = 128
HEADS_PER_GROUP = 4
N_HEADS_A = 12
N_HEADS_B = 8
WIDTH_A = N_HEADS_A * HEAD_DIM
WIDTH_A_OUT = HEADS_PER_GROUP * HEAD_DIM
WIDTH_B = N_HEADS_B * HEAD_DIM
N_IDX_HEADS = 16
IDX_DIM = 64
TOPK_MAX = 256
REL_BUCKETS = 32
REL_MAX_DIST = 2048
RMS_EPS = 1e-6
QKV_WIDTH = 3 * WIDTH_A + 3 * WIDTH_B
IDX_WIDTH = N_IDX_HEADS * IDX_DIM + IDX_DIM + N_IDX_HEADS
IDX_PAD = 1152
KI_COL = N_IDX_HEADS * IDX_DIM
WI_COL = KI_COL + IDX_DIM
LANES = 128
QB = 128
KCHUNK = 4
QT = KCHUNK * QB
QTILE = 2 * QT
KTILE = 2 * QT
TS = BAND * max(DILATIONS)
NEG = -(2.0 ** 100)
INT_MIN = -(2 ** 31)
LOG2E = math.log2(math.e)
SPARSE_Q_SCALE = HEAD_DIM ** -0.5 * LOG2E
ONES_ROWS = 16
VMEM_LIMIT = 56 * 1024 * 1024


def _params(sem):
    return pltpu.CompilerParams(dimension_semantics=sem, vmem_limit_bytes=VMEM_LIMIT)


def _sigmoid(x):
    return 0.5 * jnp.tanh(0.5 * x) + 0.5


class Weight(NamedTuple):
    stacked: jax.Array
    layer: int
    col0: int
    cols: int

    def spec(self, tn):
        assert self.col0 % tn == 0 and self.cols % tn == 0
        first = self.col0 // tn
        return pl.BlockSpec((None, self.stacked.shape[1], tn),
                            lambda i, j, layer=self.layer: (layer, 0, first + j))


def _whole(stacked, layer):
    return Weight(stacked, layer, 0, stacked.shape[2])


def _rmsnorm_kernel(x_ref, g_ref, o_ref):
    x = x_ref[...]
    y = x * lax.rsqrt(jnp.mean(x * x, axis=-1, keepdims=True) + RMS_EPS)
    o_ref[...] = (y * g_ref[...]).astype(o_ref.dtype)


def _rmsnorm(x2, g, *, tm, out_dtype, name):
    M, D = x2.shape
    return pl.pallas_call(
        _rmsnorm_kernel,
        grid=(M // tm,),
        in_specs=[pl.BlockSpec((tm, D), lambda i: (i, 0)), pl.BlockSpec((1, D), lambda i: (0, 0))],
        out_specs=pl.BlockSpec((tm, D), lambda i: (i, 0)),
        out_shape=jax.ShapeDtypeStruct((M, D), out_dtype),
        compiler_params=_params(("parallel",)),
        name=name,
    )(x2, g.reshape(1, D))


def _matmul_kernel(a_ref, w_ref, o_ref, *, layout, lead_tiles, lead_scale):
    acc = jnp.dot(a_ref[...], w_ref[...].astype(BF16), preferred_element_type=F32)
    if lead_tiles:
        acc = acc * jnp.where(pl.program_id(1) < lead_tiles, lead_scale, 1.0)
    if layout == "rows":
        o_ref[...] = acc.astype(o_ref.dtype)
    elif layout == "sigmoid":
        o_ref[...] = _sigmoid(acc).astype(o_ref.dtype)
    else:
        for c in range(acc.shape[1] // HEAD_DIM):
            head = acc[:, c * HEAD_DIM:(c + 1) * HEAD_DIM]
            if layout == "head":
                o_ref[0, c] = head.astype(o_ref.dtype)
            else:
                for r in range(acc.shape[0] // QT):
                    o_ref[0, c, r] = head[r * QT:(r + 1) * QT].T.astype(o_ref.dtype)


def _matmul(a, w, *, tm, tn, out_dtype, layout="rows", batch_seq=None, lead_cols=0,
            lead_scale=1.0, name):
    assert lead_cols % tn == 0
    M, K = a.shape
    N = w.cols
    if layout in ("rows", "sigmoid"):
        out_shape = jax.ShapeDtypeStruct((M, N), out_dtype)
        out_spec = pl.BlockSpec((tm, tn), lambda i, j: (i, j))
    else:
        B, S = batch_seq
        per_b = S // tm
        if layout == "head":
            out_shape = jax.ShapeDtypeStruct((B, N // HEAD_DIM, S, HEAD_DIM), out_dtype)
            out_spec = pl.BlockSpec((1, tn // HEAD_DIM, tm, HEAD_DIM),
                                    lambda i, j: (i // per_b, j, i % per_b, 0))
        else:
            out_shape = jax.ShapeDtypeStruct((B, N // HEAD_DIM, S // QT, HEAD_DIM, QT), out_dtype)
            out_spec = pl.BlockSpec((1, tn // HEAD_DIM, tm // QT, HEAD_DIM, QT),
                                    lambda i, j: (i // per_b, j, i % per_b, 0, 0))
    return pl.pallas_call(
        functools.partial(_matmul_kernel, layout=layout, lead_tiles=lead_cols // tn,
                          lead_scale=lead_scale),
        grid=(M // tm, N // tn),
        in_specs=[pl.BlockSpec((tm, K), lambda i, j: (i, 0)), w.spec(tn)],
        out_specs=out_spec,
        out_shape=out_shape,
        compiler_params=_params(("parallel", "parallel")),
        name=name,
    )(a, w.stacked)


def _t5_bucket(dist):
    n = jnp.maximum(dist, 0)
    max_exact = REL_BUCKETS // 2
    nf = jnp.maximum(n, 1).astype(F32)
    large = max_exact + (jnp.log(nf / max_exact) / math.log(REL_MAX_DIST / max_exact)
                         * (REL_BUCKETS - max_exact)).astype(jnp.int32)
    large = jnp.minimum(large, REL_BUCKETS - 1)
    return jnp.where(n < max_exact, n, large)


def _table_lookup(tab_ref, bucket, col):
    val = jnp.zeros(bucket.shape, F32)
    for b in range(REL_BUCKETS):
        val = jnp.where(bucket == b, tab_ref[b, col], val)
    return val


def _band_bias_kernel(tab_ref, o_ref):
    h = pl.program_id(0)
    dilation = jnp.left_shift(1, 2 * (h // HEADS_PER_GROUP))
    qi = lax.broadcasted_iota(jnp.int32, (BAND, 2 * BAND), 0)
    kj = lax.broadcasted_iota(jnp.int32, (BAND, 2 * BAND), 1)
    delta = qi + BAND - kj
    bias = _table_lookup(tab_ref, _t5_bucket(delta * dilation), h)
    o_ref[0] = jnp.where((delta >= 0) & (delta <= BAND), bias, -jnp.inf)


def _toeplitz_bias_kernel(tab_ref, o_ref):
    h = pl.program_id(0)
    kj = lax.broadcasted_iota(jnp.int32, (QB, QB), 0)
    qi = lax.broadcasted_iota(jnp.int32, (QB, QB), 1)

    def body(d, carry):
        dist = d * QB + qi - kj
        o_ref[0, d] = _table_lookup(tab_ref, _t5_bucket(dist), N_HEADS_A + h) * LOG2E
        return carry

    lax.fori_loop(0, o_ref.shape[1], body, 0)


def _bias_tiles(rel_bias, n_blocks):
    smem = pl.BlockSpec(memory_space=pltpu.SMEM)
    band = pl.pallas_call(
        _band_bias_kernel,
        grid=(N_HEADS_A,),
        in_specs=[smem],
        out_specs=pl.BlockSpec((1, BAND, 2 * BAND), lambda h: (h, 0, 0)),
        out_shape=jax.ShapeDtypeStruct((N_HEADS_A, BAND, 2 * BAND), F32),
        name="band_bias",
    )(rel_bias)
    toep = pl.pallas_call(
        _toeplitz_bias_kernel,
        grid=(N_HEADS_B,),
        in_specs=[smem],
        out_specs=pl.BlockSpec((1, n_blocks, QB, QB), lambda h: (h, 0, 0, 0)),
        out_shape=jax.ShapeDtypeStruct((N_HEADS_B, n_blocks, QB, QB), F32),
        name="toeplitz_bias",
    )(rel_bias)
    return band, toep


def _dilated_kernel(*refs):
    n_g = len(DILATIONS)
    groups = [refs[6 * g:6 * g + 6] for g in range(n_g)]
    y_ref, o_scr, l_scr = refs[6 * n_g:]
    i = pl.program_id(2)
    scale = HEAD_DIM ** -0.5
    col = lax.broadcasted_iota(jnp.int32, (BAND, 2 * BAND), 1)

    for g, d in enumerate(DILATIONS):
        q_ref, k_ref, v_ref, kp_ref, vp_ref, bias_ref = groups[g]
        span = d * BAND
        bias = bias_ref[0]
        bias_first = jnp.where((i > 0) | (col >= BAND), bias, -jnp.inf)
        for r in range(d):
            prev = pl.ds(r, BAND, stride=d)
            k_prev, v_prev = kp_ref[0, 0, prev, :].astype(BF16), vp_ref[0, 0, prev, :].astype(BF16)
            for n in range(TS // span):
                rows = pl.ds(n * span + r, BAND, stride=d)
                b = bias_first if n == 0 else bias
                k_cur, v_cur = k_ref[0, 0, rows, :].astype(BF16), v_ref[0, 0, rows, :].astype(BF16)
                kwin = jnp.concatenate([k_prev, k_cur], axis=0)
                vwin = jnp.concatenate([v_prev, v_cur], axis=0)
                k_prev, v_prev = k_cur, v_cur
                q = q_ref[0, 0, rows, :].astype(BF16)
                s = lax.dot_general(q, kwin, (((1,), (1,)), ((), ())), preferred_element_type=F32)
                s = s * scale + b
                m = jnp.max(s, axis=-1, keepdims=True)
                p = jnp.exp(s - m)
                l = jnp.sum(p, axis=-1, keepdims=True)
                o_scr[g, rows, :] = jnp.dot(p.astype(BF16), vwin, preferred_element_type=F32) / l
                l_scr[g, rows, :] = jnp.broadcast_to(m + jnp.log(l), (BAND, HEAD_DIM))

    lses = [l_scr[g] for g in range(n_g)]
    mx = functools.reduce(jnp.maximum, lses)
    es = [jnp.exp(lse - mx) for lse in lses]
    tot = functools.reduce(jnp.add, es)
    y = functools.reduce(jnp.add, [(e / tot) * o_scr[g] for g, e in enumerate(es)])
    y_ref[...] = y.astype(y_ref.dtype)


def _dilated_attention(qkv_a, band_bias):
    B, _, S, _ = qkv_a.shape
    tiles = S // TS
    in_specs, args = [], []
    for g, d in enumerate(DILATIONS):
        span = d * BAND
        per_tile = TS // span
        for base in (0, N_HEADS_A, 2 * N_HEADS_A):
            in_specs.append(pl.BlockSpec(
                (1, 1, TS, HEAD_DIM),
                lambda b, h, i, hd=base + g * HEADS_PER_GROUP: (b, hd + h, i, 0)))
        for base in (N_HEADS_A, 2 * N_HEADS_A):
            in_specs.append(pl.BlockSpec(
                (1, 1, span, HEAD_DIM),
                lambda b, h, i, hd=base + g * HEADS_PER_GROUP, pt=per_tile:
                (b, hd + h, jnp.maximum(i * pt - 1, 0), 0)))
        in_specs.append(pl.BlockSpec((1, BAND, 2 * BAND),
                                     lambda b, h, i, g=g: (g * HEADS_PER_GROUP + h, 0, 0)))
        args += [qkv_a] * 5 + [band_bias]
    return pl.pallas_call(
        _dilated_kernel,
        grid=(B, HEADS_PER_GROUP, tiles),
        in_specs=in_specs,
        out_specs=pl.BlockSpec((TS, HEAD_DIM), lambda b, h, i: (b * tiles + i, h)),
        out_shape=jax.ShapeDtypeStruct((B * S, WIDTH_A_OUT), BF16),
        scratch_shapes=[pltpu.VMEM((len(DILATIONS), TS, HEAD_DIM), F32),
                        pltpu.VMEM((len(DILATIONS), TS, HEAD_DIM), F32)],
        compiler_params=_params(("parallel", "parallel", "parallel")),
        name="dilated_attn",
    )(*args)


def _indexer_kernel(q_ref, k_ref, mask_ref, qh_ref, wb_ref, key_ref, cut_ref, thr_ref, cnt_ref,
                    *, n_kb, topk):
    i = pl.program_id(1)
    n_vis = i + 1
    n_chunks = (i + KCHUNK) // KCHUNK
    sub = QB // 8
    for h in range(N_IDX_HEADS):
        qh_ref[h * QB:(h + 1) * QB, :] = q_ref[0, :, h * IDX_DIM:(h + 1) * IDX_DIM].astype(BF16)
    wt = q_ref[0, :, KI_COL:KI_COL + LANES].T * ((N_IDX_HEADS ** -0.5) * (IDX_DIM ** -0.5))
    for h in range(N_IDX_HEADS):
        wb_ref[h] = jnp.broadcast_to(wt[IDX_DIM + h:IDX_DIM + h + 1, :], (8, QB))

    key_pos = (lax.broadcasted_iota(jnp.int32, (sub, 8, QB), 0) * 8
               + lax.broadcasted_iota(jnp.int32, (sub, 8, QB), 1))
    diag_vis = key_pos <= lax.broadcasted_iota(jnp.int32, (sub, 8, QB), 2)

    def score_chunk(c):
        kchunk = k_ref[0, c * QT:(c + 1) * QT, 0:IDX_DIM].astype(BF16)
        accs = [jnp.zeros((sub, 8, QB), F32) for _ in range(KCHUNK)]
        for pair in range(N_IDX_HEADS // 2):
            d = lax.dot_general(kchunk, qh_ref[pair * 2 * QB:(pair + 1) * 2 * QB, :],
                                (((1,), (1,)), ((), ())), preferred_element_type=F32)
            for u in range(2):
                w = wb_ref[2 * pair + u][None]
                for t in range(KCHUNK):
                    tile = d[t * QB:(t + 1) * QB, u * QB:(u + 1) * QB].reshape(sub, 8, QB)
                    accs[t] = accs[t] + jnp.maximum(tile, 0.0) * w
        for t in range(KCHUNK):
            kb = c * KCHUNK + t
            vis = (kb < i) | ((kb == i) & diag_vis)
            bits = pltpu.bitcast(jnp.where(vis, accs[t], -jnp.inf), jnp.int32)
            key_ref[kb * sub:(kb + 1) * sub] = bits ^ ((bits >> 31) & 0x7FFFFFFF)

    def count(pred):
        def count_body(c, cnt):
            for u in range(KCHUNK):
                kb = c * KCHUNK + u
                keys = key_ref[pl.ds(pl.multiple_of(kb * sub, sub), sub)]
                cnt = cnt + jnp.sum(pred(keys, kb).astype(jnp.int32), axis=0)
            return cnt

        cnt = lax.fori_loop(0, n_chunks, count_body, jnp.zeros((8, QB), jnp.int32))
        return jnp.sum(cnt, axis=0, keepdims=True)

    def rows(v):
        return jnp.broadcast_to(v, (8, QB))[None]

    n_keys = n_kb * QB
    for visited in range(1, n_kb // KCHUNK + 1):
        @pl.when(n_chunks == visited)
        def _(visited=visited):
            for c in range(visited):
                score_chunk(c)

    for blocks in range(1, n_kb + 1):
        @pl.when(n_vis == blocks)
        def _(blocks=blocks):
            def bit_body(it, state):
                t, cnt_t = state
                cand = t + jnp.left_shift(1, 31 - it)
                cand_b = rows(cand)
                cnt = jnp.zeros((8, QB), jnp.int32)
                for lo in range(0, blocks, KCHUNK):
                    keys = key_ref[lo * sub:min(lo + KCHUNK, blocks) * sub]
                    cnt = cnt + jnp.sum((keys >= cand_b).astype(jnp.int32), axis=0)
                total = jnp.sum(cnt, axis=0, keepdims=True)
                take = total >= topk
                return jnp.where(take, cand, t), jnp.where(take, total, cnt_t)

            thr, cnt_thr = lax.fori_loop(
                0, 32, bit_body,
                (jnp.full((1, QB), INT_MIN, jnp.int32),
                 jnp.full((1, QB), n_keys + 1, jnp.int32)))
            thr_ref[...] = thr
            cnt_ref[...] = cnt_thr

    thr, cnt_thr = thr_ref[...], cnt_ref[...]
    thr_b = rows(thr)

    cut_ref[...] = jnp.full((1, QB), n_keys, jnp.int32)

    @pl.when(jnp.max(cnt_thr) > topk)
    def _():
        need = topk - count(lambda keys, kb: keys > thr_b)

        def pos_body(j, u):
            cand = u + jnp.left_shift(1, n_keys.bit_length() - 2 - j)
            cand_b = rows(cand)
            below = count(lambda keys, kb: (keys == thr_b) & (kb * QB + key_pos < cand_b))
            return jnp.where(below < need, cand, u)

        cut_ref[...] = lax.fori_loop(0, n_keys.bit_length() - 1, pos_body,
                                     jnp.zeros((1, QB), jnp.int32))

    cut_b = rows(cut_ref[...])

    def mask_body(kb, carry):
        keys = key_ref[pl.ds(pl.multiple_of(kb * sub, sub), sub)]
        sel = (keys > thr_b) | ((keys == thr_b) & (kb * QB + key_pos <= cut_b))
        sel = sel & ((kb < i) | diag_vis)
        mask_ref[0, 0, kb] = jnp.where(sel, 0.0, NEG).reshape(QB, QB).astype(mask_ref.dtype)
        return carry

    lax.fori_loop(0, n_vis, mask_body, 0)

    def zero_body(kb, carry):
        mask_ref[0, 0, kb] = jnp.full((QB, QB), NEG, mask_ref.dtype)
        return carry

    lax.fori_loop(n_vis, n_kb, zero_body, 0)


def _indexer_mask(idx, topk):
    B, S, _ = idx.shape
    n_kb = S // QB
    return pl.pallas_call(
        functools.partial(_indexer_kernel, n_kb=n_kb, topk=topk),
        grid=(B, n_kb),
        in_specs=[pl.BlockSpec((1, QB, IDX_PAD), lambda b, i: (b, i, 0)),
                  pl.BlockSpec((1, S, LANES), lambda b, i: (b, 0, KI_COL // LANES))],
        out_specs=pl.BlockSpec((1, 1, n_kb, QB, QB), lambda b, i: (b, i, 0, 0, 0)),
        out_shape=jax.ShapeDtypeStruct((B, n_kb, n_kb, QB, QB), BF16),
        scratch_shapes=[pltpu.VMEM((N_IDX_HEADS * QB, IDX_DIM), BF16),
                        pltpu.VMEM((N_IDX_HEADS, 8, QB), F32),
                        pltpu.VMEM((n_kb * QB // 8, 8, QB), jnp.int32)]
        + [pltpu.VMEM((1, QB), jnp.int32)] * 3,
        compiler_params=_params(("parallel", "parallel")),
        name="indexer_topk_mask",
    )(idx, idx)


def _sparse_attn_kernel(q_ref, k_ref, vt_ref, bias_ref, mask_ref, o_ref, m_ref, acc_ref, *,
                        n_tiles):
    i = pl.program_id(2)
    q = q_ref[0, 0]
    m_ref[...] = jnp.full(m_ref.shape, NEG, F32)
    acc_ref[...] = jnp.zeros(acc_ref.shape, F32)
    ones = jnp.ones((ONES_ROWS, KTILE), BF16)
    q_blocks = QTILE // QB
    k_blocks = KTILE // QB

    def step(c, tile):
        s = lax.dot_general(k_ref[0, 0, c * KTILE:(c + 1) * KTILE, :], q, (((1,), (1,)), ((), ())),
                            preferred_element_type=F32)
        rows = []
        for t in range(k_blocks):
            parts = []
            for a in range(q_blocks):
                bias = bias_ref[0, max(tile * q_blocks + a - (c * k_blocks + t), 0)]
                drop = mask_ref[0, a, c * k_blocks + t].astype(F32)
                parts.append(s[t * QB:(t + 1) * QB, a * QB:(a + 1) * QB] + bias + drop)
            rows.append(jnp.concatenate(parts, axis=1))
        s = jnp.concatenate(rows, axis=0)
        m_old = m_ref[...]
        m_new = jnp.maximum(m_old, jnp.max(s, axis=0, keepdims=True))
        alpha = jnp.exp2(m_old - m_new)
        p = jnp.exp2(s - m_new)
        m_ref[...] = m_new
        vt = [vt_ref[0, 0, c * (KTILE // QT) + j] for j in range(KTILE // QT)]
        v_ones = jnp.concatenate([jnp.concatenate(vt, axis=1), ones], axis=0)
        acc_ref[...] = alpha * acc_ref[...] + jnp.dot(v_ones, p.astype(BF16),
                                                      preferred_element_type=F32)

    for tile in range(n_tiles):
        @pl.when(i == tile)
        def _(tile=tile):
            for c in range((tile + 1) * (QTILE // KTILE)):
                step(c, tile)

    out = acc_ref[0:HEAD_DIM, :] / acc_ref[HEAD_DIM:HEAD_DIM + 1, :]
    o_ref[0] = out.T.astype(o_ref.dtype)


def _sparse_attention(qk, vt, toep_bias, mask):
    B, _, S, _ = qk.shape
    n_kb = S // QB
    return pl.pallas_call(
        functools.partial(_sparse_attn_kernel, n_tiles=S // QTILE),
        grid=(B, N_HEADS_B, S // QTILE),
        in_specs=[pl.BlockSpec((1, 1, QTILE, HEAD_DIM), lambda b, h, i: (b, h, i, 0)),
                  pl.BlockSpec((1, 1, S, HEAD_DIM), lambda b, h, i: (b, N_HEADS_B + h, 0, 0)),
                  pl.BlockSpec((1, 1, S // QT, HEAD_DIM, QT), lambda b, h, i: (b, h, 0, 0, 0)),
                  pl.BlockSpec((1, n_kb, QB, QB), lambda b, h, i: (h, 0, 0, 0)),
                  pl.BlockSpec((1, QTILE // QB, n_kb, QB, QB), lambda b, h, i: (b, i, 0, 0, 0))],
        out_specs=pl.BlockSpec((1, QTILE, HEAD_DIM), lambda b, h, i: (b, i, h)),
        out_shape=jax.ShapeDtypeStruct((B, S, WIDTH_B), BF16),
        scratch_shapes=[pltpu.VMEM((1, QTILE), F32),
                        pltpu.VMEM((HEAD_DIM + ONES_ROWS, QTILE), F32)],
        compiler_params=_params(("parallel", "parallel", "parallel")),
        name="sparse_attn",
    )(qk, qk, vt, toep_bias, mask)


def _merge_kernel(ya_ref, yb_ref, ga_ref, gb_ref, wa_ref, wb_ref, out_ref):
    a = jnp.dot(ya_ref[...], wa_ref[...].astype(BF16), preferred_element_type=F32)
    b = jnp.dot(yb_ref[...], wb_ref[...].astype(BF16), preferred_element_type=F32)
    out_ref[...] = (ga_ref[...].astype(F32) * a + gb_ref[...].astype(F32) * b).astype(out_ref.dtype)


def _merge(ya, yb, gates, wa, wb, *, tm, tn):
    M = yb.shape[0]
    N = wa.cols
    return pl.pallas_call(
        _merge_kernel,
        grid=(M // tm, N // tn),
        in_specs=[pl.BlockSpec((tm, WIDTH_A_OUT), lambda i, j: (i, 0)),
                  pl.BlockSpec((tm, WIDTH_B), lambda i, j: (i, 0)),
                  pl.BlockSpec((tm, tn), lambda i, j: (i, j)),
                  pl.BlockSpec((tm, tn), lambda i, j: (i, j + N // tn)),
                  wa.spec(tn), wb.spec(tn)],
        out_specs=pl.BlockSpec((tm, tn), lambda i, j: (i, j)),
        out_shape=jax.ShapeDtypeStruct((M, N), BF16),
        compiler_params=_params(("parallel", "parallel")),
        name="gated_merge",
    )(ya, yb, gates, gates, wa.stacked, wb.stacked)


def _matmul_residual_kernel(a_ref, w_ref, r_ref, o_ref):
    o_ref[...] = r_ref[...] + jnp.dot(a_ref[...], w_ref[...].astype(BF16),
                                      preferred_element_type=F32)


def _matmul_residual(a, w, r, *, tm, tn, name):
    M, K = a.shape
    N = w.cols
    return pl.pallas_call(
        _matmul_residual_kernel,
        grid=(M // tm, N // tn),
        in_specs=[pl.BlockSpec((tm, K), lambda i, j: (i, 0)), w.spec(tn),
                  pl.BlockSpec((tm, tn), lambda i, j: (i, j))],
        out_specs=pl.BlockSpec((tm, tn), lambda i, j: (i, j)),
        out_shape=jax.ShapeDtypeStruct((M, N), F32),
        compiler_params=_params(("parallel", "parallel")),
        name=name,
    )(a, w.stacked, r)


def _residual_norm_kernel(a_ref, w_ref, r_ref, g_ref, x_ref, h_ref):
    x = r_ref[...] + jnp.dot(a_ref[...], w_ref[...].astype(BF16), preferred_element_type=F32)
    x_ref[...] = x
    y = x * lax.rsqrt(jnp.mean(x * x, axis=-1, keepdims=True) + RMS_EPS)
    h_ref[...] = (y * g_ref[...]).astype(h_ref.dtype)


def _matmul_residual_norm(a, w, r, g, *, tm, name):
    M, K = a.shape
    N = w.cols
    whole_rows = pl.BlockSpec((tm, N), lambda i, j: (i, 0))
    return pl.pallas_call(
        _residual_norm_kernel,
        grid=(M // tm, 1),
        in_specs=[pl.BlockSpec((tm, K), lambda i, j: (i, 0)), w.spec(N), whole_rows,
                  pl.BlockSpec((1, N), lambda i, j: (0, 0))],
        out_specs=[whole_rows, whole_rows],
        out_shape=[jax.ShapeDtypeStruct((M, N), F32), jax.ShapeDtypeStruct((M, N), BF16)],
        compiler_params=_params(("parallel", "arbitrary")),
        name=name,
    )(a, w.stacked, r, g.reshape(1, N))


def _swiglu_kernel(h_ref, wg_ref, wu_ref, o_ref):
    h = h_ref[...]
    gate = jnp.dot(h, wg_ref[...].astype(BF16), preferred_element_type=F32)
    up = jnp.dot(h, wu_ref[...].astype(BF16), preferred_element_type=F32)
    o_ref[...] = (gate * _sigmoid(gate) * up).astype(o_ref.dtype)


def _swiglu(h, wg, wu, *, tm, tn):
    M, D = h.shape
    N = wg.cols
    return pl.pallas_call(
        _swiglu_kernel,
        grid=(M // tm, N // tn),
        in_specs=[pl.BlockSpec((tm, D), lambda i, j: (i, 0)), wg.spec(tn), wu.spec(tn)],
        out_specs=pl.BlockSpec((tm, tn), lambda i, j: (i, j)),
        out_shape=jax.ShapeDtypeStruct((M, N), BF16),
        compiler_params=_params(("parallel", "parallel")),
        name="swiglu_gate_up",
    )(h, wg.stacked, wu.stacked)


def kernel(x, w_in, w_out_a, w_out_b, w_out, w_ff_gate, w_ff_up, w_ff_down,
           g_mix, g_ffn, g_final, rel_bias):
    B, S, D = x.shape
    depth = w_in.shape[0]
    M = B * S
    assert S % TS == 0 and S % QTILE == 0
    assert w_in.shape[2] == QKV_WIDTH + IDX_WIDTH + 2 * D
    topk = min(TOPK_MAX, S // 4)
    tm, tm_small, tn = 2048, 1024, 512

    band_bias, toep_bias = _bias_tiles(rel_bias, S // QB)
    x2 = x.reshape(M, D)
    h = _rmsnorm(x2, g_mix[0], tm=tm_small, out_dtype=BF16, name="rmsnorm_mix")
    for layer in range(depth):
        wl = w_in[layer]
        w_qkv = wl[:, :QKV_WIDTH].astype(BF16)[None]
        w_a = Weight(w_qkv, 0, 0, 3 * WIDTH_A)
        w_qkb = Weight(w_qkv, 0, 3 * WIDTH_A, 2 * WIDTH_B)
        w_vb = Weight(w_qkv, 0, QKV_WIDTH - WIDTH_B, WIDTH_B)
        w_idx = _whole(jnp.pad(wl[:, QKV_WIDTH:QKV_WIDTH + IDX_WIDTH],
                               ((0, 0), (0, IDX_PAD - IDX_WIDTH))).astype(BF16)[None], 0)
        w_gate = _whole(wl[:, QKV_WIDTH + IDX_WIDTH:].astype(BF16)[None], 0)

        qkv_a = _matmul(h, w_a, tm=tm, tn=tn, out_dtype=F32, layout="head",
                        batch_seq=(B, S), name="proj_a")
        qk_b = _matmul(h, w_qkb, tm=tm, tn=tn, out_dtype=BF16, layout="head", batch_seq=(B, S),
                       lead_cols=WIDTH_B, lead_scale=SPARSE_Q_SCALE, name="proj_qk_b")
        vt = _matmul(h, w_vb, tm=tm_small, tn=tn, out_dtype=BF16, layout="head_t",
                     batch_seq=(B, S), name="proj_vt")
        idx = _matmul(h, w_idx, tm=tm_small, tn=IDX_PAD, out_dtype=F32, name="proj_idx")
        gates = _matmul(h, w_gate, tm=tm, tn=tn, out_dtype=BF16, layout="sigmoid",
                        name="proj_gates")

        y_a = _dilated_attention(qkv_a, band_bias)
        mask = _indexer_mask(idx.reshape(B, S, IDX_PAD), topk)
        y_b = _sparse_attention(qk_b, vt, toep_bias, mask).reshape(M, WIDTH_B)

        merged = _merge(y_a, y_b, gates, _whole(w_out_a, layer), _whole(w_out_b, layer),
                        tm=tm, tn=tn)
        x2, h = _matmul_residual_norm(merged, _whole(w_out[layer].astype(BF16)[None], 0), x2,
                                      g_ffn[layer], tm=tm_small // 2, name="out_proj_norm")
        act = _swiglu(h, _whole(w_ff_gate, layer), _whole(w_ff_up, layer), tm=tm, tn=tn)
        x2 = _matmul_residual(act, _whole(w_ff_down[layer].astype(BF16)[None], 0), x2,
                              tm=tm_small, tn=tn, name="ffn_down")
        if layer + 1 < depth:
            h = _rmsnorm(x2, g_mix[layer + 1], tm=tm_small, out_dtype=BF16, name="rmsnorm_mix")
    return _rmsnorm(x2, g_final, tm=tm_small, out_dtype=F32, name="final_rmsnorm").reshape(B, S, D)
```
